```python
import jax, jax.numpy as jnp
from jax import lax
import numpy as np

D_MODEL = 4096
BATCH = 4
SEQ = 2048
DEPTH = 2

N_MIXERS = 2
EXPAND = 2
D_INNER = EXPAND * D_MODEL
CHUNK = 128
A_HEADS = 16
A_HEAD_DIM = D_INNER // A_HEADS
POOL_WINDOWS = (2, 4, 8, 16)
B_GROUPS = len(POOL_WINDOWS)
B_GROUP_DIM = D_INNER // B_GROUPS
N_A_LAYERS = (DEPTH + 1) // 2
N_B_LAYERS = DEPTH // 2
EPS = 1e-6

kernel_name = "hybrid_gmlp_pool_sandwich_trunk"


def rms_norm(x, g):
    x32 = x.astype(jnp.float32)
    y = x32 * lax.rsqrt(jnp.mean(x32 * x32, axis=-1, keepdims=True) + EPS)
    return (y * g.astype(jnp.float32)).astype(x.dtype)


def layer_norm(x, g, b):
    x32 = x.astype(jnp.float32)
    mu = jnp.mean(x32, axis=-1, keepdims=True)
    xc = x32 - mu
    y = xc * lax.rsqrt(jnp.mean(xc * xc, axis=-1, keepdims=True) + EPS)
    return (y * g.astype(jnp.float32) + b.astype(jnp.float32)).astype(x.dtype)


def mixer_a(h, w_in, ln_g, ln_b, w_s, b_s, w_out):
    bsz, s, _ = h.shape
    proj = h @ w_in
    u, v, z = jnp.split(proj, 3, axis=-1)
    u = jax.nn.gelu(u, approximate=False)
    v = layer_norm(jax.nn.gelu(v, approximate=False), ln_g, ln_b)
    v = v.reshape(bsz, s // CHUNK, CHUNK, A_HEADS, A_HEAD_DIM)
    w_causal = jnp.tril(w_s)
    sv = jnp.einsum('hts,bnshd->bnthd', w_causal, v) + b_s.T[:, :, None]
    sgu = u * sv.reshape(bsz, s, D_INNER)
    y = sgu * jax.nn.silu(z)
    return y @ w_out


def mixer_b(h, w_in, w_grp, b_grp, scale, w_out):
    bsz, s, _ = h.shape
    proj = h @ w_in
    p, z = jnp.split(proj, 2, axis=-1)
    p = p.reshape(bsz, s, B_GROUPS, B_GROUP_DIM)
    p32 = p.astype(jnp.float32)
    csum = jnp.cumsum(p32, axis=1)
    pos = jnp.arange(1, s + 1, dtype=jnp.float32)
    means = []
    for k, w in enumerate(POOL_WINDOWS):
        ck = csum[:, :, k]
        lagged = jnp.pad(ck, ((0, 0), (w, 0), (0, 0)))[:, :s]
        count = jnp.minimum(pos, float(w))[:, None]
        means.append((ck - lagged) / count)
    pooled = (jnp.stack(means, axis=2) - p32).astype(p.dtype)
    mixed = jnp.einsum('bsgc,gcd->bsgd', pooled, w_grp) + b_grp
    mixed = mixed.reshape(bsz, s, D_INNER) * scale
    y = mixed * jax.nn.silu(z)
    return y @ w_out


def setup_inputs(seed: int = 0) -> dict:
    key = jax.random.key(seed)
    ks = jax.random.split(key, 16)
    f32 = jnp.float32
    nrm = lambda k, shape, sc: jax.random.normal(k, shape, f32) * sc
    x = nrm(ks[0], (BATCH, SEQ, D_MODEL), 1.0)
    pre_norm = 1.0 + nrm(ks[1], (DEPTH, D_MODEL), 0.02)
    post_norm = 1.0 + nrm(ks[2], (DEPTH, D_MODEL), 0.02)
    a_w_in = nrm(ks[3], (N_A_LAYERS, D_MODEL, 3 * D_INNER), D_MODEL ** -0.5)
    a_ln_g = 1.0 + nrm(ks[4], (N_A_LAYERS, D_INNER), 0.02)
    a_ln_b = nrm(ks[5], (N_A_LAYERS, D_INNER), 0.02)
    a_w_s = nrm(ks[6], (N_A_LAYERS, A_HEADS, CHUNK, CHUNK), CHUNK ** -0.5)
    a_b_s = 1.0 + nrm(ks[7], (N_A_LAYERS, A_HEADS, CHUNK), 0.01)
    a_w_out = nrm(ks[8], (N_A_LAYERS, D_INNER, D_MODEL), D_INNER ** -0.5)
    b_w_in = nrm(ks[9], (N_B_LAYERS, D_MODEL, 2 * D_INNER), D_MODEL ** -0.5)
    b_w_grp = nrm(ks[10], (N_B_LAYERS, B_GROUPS, B_GROUP_DIM, B_GROUP_DIM), B_GROUP_DIM ** -0.5)
    b_b_grp = nrm(ks[11], (N_B_LAYERS, B_GROUPS, B_GROUP_DIM), 0.02)
    b_scale = 1.0 + nrm(ks[12], (N_B_LAYERS, D_INNER), 0.02)
    b_w_out = nrm(ks[13], (N_B_LAYERS, D_INNER, D_MODEL), D_INNER ** -0.5)
    return {"x": x, "pre_norm": pre_norm, "post_norm": post_norm,
            "a_w_in": a_w_in, "a_ln_g": a_ln_g, "a_ln_b": a_ln_b,
            "a_w_s": a_w_s, "a_b_s": a_b_s, "a_w_out": a_w_out,
            "b_w_in": b_w_in, "b_w_grp": b_w_grp, "b_b_grp": b_b_grp,
            "b_scale": b_scale, "b_w_out": b_w_out}


def reference(x, pre_norm, post_norm, a_w_in, a_ln_g, a_ln_b, a_w_s, a_b_s, a_w_out,
              b_w_in, b_w_grp, b_b_grp, b_scale, b_w_out):
    for i in range(DEPTH):
        h = rms_norm(x, pre_norm[i])
        j = i // N_MIXERS
        if i % N_MIXERS == 0:
            m = mixer_a(h, a_w_in[j], a_ln_g[j], a_ln_b[j], a_w_s[j], a_b_s[j], a_w_out[j])
        else:
            m = mixer_b(h, b_w_in[j], b_w_grp[j], b_b_grp[j], b_scale[j], b_w_out[j])
        x = x + rms_norm(m, post_norm[i])
    return x
```

```python
import functools
import math

import jax
import jax.numpy as jnp
from jax import lax
from jax.experimental import pallas as pl
from jax.experimental.pallas import tpu as pltpu

EPS = 1e-6
CHUNK = 128
POOL_WINDOWS = (2, 4, 8, 16)
MAX_WINDOW = max(POOL_WINDOWS)
LANES = 128
MIB = 1024 * 1024

F32 = jnp.float32
BF16 = jnp.bfloat16


def _params(semantics, vmem_mib):
    return pltpu.CompilerParams(dimension_semantics=semantics,
                                vmem_limit_bytes=vmem_mib * MIB)


def _gelu(x):
    return 0.5 * x * (1.0 + lax.erf(x * math.sqrt(0.5)))


def _silu(x):
    return x / (1.0 + jnp.exp(-x))


def _dot(a, b):
    return jnp.dot(a, b, preferred_element_type=F32)


def _prenorm_kernel(x_ref, g_ref, h_ref):
    x = x_ref[...]
    ms = jnp.mean(x * x, axis=-1, keepdims=True)
    h_ref[...] = (x * lax.rsqrt(ms + EPS) * g_ref[...]).astype(h_ref.dtype)


def _prenorm(x2d, gain, tm=512):
    m, d = x2d.shape
    return pl.pallas_call(
        _prenorm_kernel,
        grid=(m // tm,),
        in_specs=[pl.BlockSpec((tm, d), lambda i: (i, 0)),
                  pl.BlockSpec((1, d), lambda i: (0, 0))],
        out_specs=pl.BlockSpec((tm, d), lambda i: (i, 0)),
        out_shape=jax.ShapeDtypeStruct((m, d), BF16),
        compiler_params=_params(("parallel",), 40),
        name="prenorm",
    )(x2d, gain.reshape(1, d))


def _a_in_kernel(h_ref, wu_ref, wv_ref, wz_ref, a_ref, g_ref, s1_ref, s2_ref):
    j = pl.program_id(1)
    h = h_ref[...]
    u = _dot(h, wu_ref[...])
    v = _dot(h, wv_ref[...])
    z = _dot(h, wz_ref[...])
    a_ref[...] = (_gelu(u) * _silu(z)).astype(a_ref.dtype)
    gv = _gelu(v)
    g_ref[...] = gv.astype(g_ref.dtype)
    tn = gv.shape[1]
    p1 = gv[:, 0:LANES]
    p2 = p1 * p1
    for k in range(1, tn // LANES):
        blk = gv[:, k * LANES:(k + 1) * LANES]
        p1 = p1 + blk
        p2 = p2 + blk * blk

    @pl.when(j == 0)
    def _():
        s1_ref[...] = p1
        s2_ref[...] = p2

    @pl.when(j > 0)
    def _():
        s1_ref[...] += p1
        s2_ref[...] += p2


def _a_in(h, w_in, e, tm=1024, tn=256):
    m, d = h.shape
    nj = e // tn
    row = lambda i, j: (i, 0)
    return pl.pallas_call(
        _a_in_kernel,
        grid=(m // tm, nj),
        in_specs=[pl.BlockSpec((tm, d), row),
                  pl.BlockSpec((d, tn), lambda i, j: (0, j)),
                  pl.BlockSpec((d, tn), lambda i, j: (0, nj + j)),
                  pl.BlockSpec((d, tn), lambda i, j: (0, 2 * nj + j))],
        out_specs=[pl.BlockSpec((tm, tn), lambda i, j: (i, j)),
                   pl.BlockSpec((tm, tn), lambda i, j: (i, j)),
                   pl.BlockSpec((tm, LANES), row),
                   pl.BlockSpec((tm, LANES), row)],
        out_shape=[jax.ShapeDtypeStruct((m, e), BF16),
                   jax.ShapeDtypeStruct((m, e), BF16),
                   jax.ShapeDtypeStruct((m, LANES), F32),
                   jax.ShapeDtypeStruct((m, LANES), F32)],
        compiler_params=_params(("parallel", "arbitrary"), 56),
        name="a_in",
    )(h, w_in, w_in, w_in)


OUT_COL_SPLIT = 4
EPILOGUE_ROWS = 64


def _accumulate_out_projection(y, wo_ref, x_ref, pg_ref, o_ref, *, first, last):
    tm, d = o_ref.shape
    dn = d // OUT_COL_SPLIT

    @pl.when(first)
    def _():
        o_ref[...] = jnp.zeros(o_ref.shape, o_ref.dtype)

    for nb in range(OUT_COL_SPLIT):
        cols = slice(nb * dn, (nb + 1) * dn)
        o_ref[:, cols] += _dot(y, wo_ref[:, cols])

    @pl.when(last)
    def _():
        def body(r, carry):
            rows = pl.ds(pl.multiple_of(r * EPILOGUE_ROWS, EPILOGUE_ROWS), EPILOGUE_ROWS)
            mm = o_ref[rows, :]
            ms = jnp.mean(mm * mm, axis=-1, keepdims=True)
            o_ref[rows, :] = x_ref[rows, :] + mm * lax.rsqrt(ms + EPS) * pg_ref[...]
            return carry
        lax.fori_loop(0, tm // EPILOGUE_ROWS, body, 0)


def _a_out_kernel(a_ref, g_ref, s1_ref, s2_ref, lng_ref, lnb_ref, ws_ref, bs_ref,
                  wo_ref, x_ref, pg_ref, o_ref, *, e):
    k = pl.program_id(1)
    tm = a_ref.shape[0]
    mu = jnp.sum(s1_ref[...], axis=-1, keepdims=True) * (1.0 / e)
    var = jnp.sum(s2_ref[...], axis=-1, keepdims=True) * (1.0 / e) - mu * mu
    rstd = lax.rsqrt(var + EPS)
    g = g_ref[...].astype(F32)
    vn = ((g - mu) * rstd * lng_ref[...] + lnb_ref[...]).astype(BF16)
    w = ws_ref[0]
    t_idx = lax.broadcasted_iota(jnp.int32, w.shape, 0)
    s_idx = lax.broadcasted_iota(jnp.int32, w.shape, 1)
    wc = jnp.where(t_idx >= s_idx, w, 0.0).astype(BF16)
    bs = bs_ref[0]
    parts = []
    for c in range(tm // CHUNK):
        rows = slice(c * CHUNK, (c + 1) * CHUNK)
        sv = _dot(wc, vn[rows, :]) + bs
        parts.append((a_ref[rows, :].astype(F32) * sv).astype(BF16))
    y = jnp.concatenate(parts, axis=0)
    _accumulate_out_projection(y, wo_ref, x_ref, pg_ref, o_ref,
                               first=k == 0, last=k == pl.num_programs(1) - 1)


def _a_out(a, g, s1, s2, ln_g, ln_b, w_s, b_s, w_out, x2d, post_gain, tm=512):
    m, e = a.shape
    d = w_out.shape[1]
    heads = w_s.shape[0]
    hd = e // heads
    row = lambda i, k: (i, 0)
    col = lambda i, k: (0, k)
    return pl.pallas_call(
        functools.partial(_a_out_kernel, e=e),
        grid=(m // tm, heads),
        in_specs=[pl.BlockSpec((tm, hd), lambda i, k: (i, k)),
                  pl.BlockSpec((tm, hd), lambda i, k: (i, k)),
                  pl.BlockSpec((tm, LANES), row),
                  pl.BlockSpec((tm, LANES), row),
                  pl.BlockSpec((1, hd), col),
                  pl.BlockSpec((1, hd), col),
                  pl.BlockSpec((1, CHUNK, CHUNK), lambda i, k: (k, 0, 0)),
                  pl.BlockSpec((1, CHUNK, 1), lambda i, k: (k, 0, 0)),
                  pl.BlockSpec((hd, d), lambda i, k: (k, 0)),
                  pl.BlockSpec((tm, d), row, pipeline_mode=pl.Buffered(1)),
                  pl.BlockSpec((1, d), lambda i, k: (0, 0))],
        out_specs=pl.BlockSpec((tm, d), row),
        out_shape=jax.ShapeDtypeStruct((m, d), F32),
        compiler_params=_params(("parallel", "arbitrary"), 56),
        name="a_out",
    )(a, g, s1, s2, ln_g.reshape(1, e), ln_b.reshape(1, e), w_s,
      b_s.reshape(heads, CHUNK, 1), w_out, x2d, post_gain.reshape(1, d))


def _b_in_kernel(h_ref, wp_ref, wz_ref, pooled_ref, sz_ref, ext_ref, *, seq, group_dim):
    j = pl.program_id(0)
    i = pl.program_id(1)
    tm, tn = pooled_ref.shape
    h = h_ref[...]
    p = _dot(h, wp_ref[...])
    z = _dot(h, wz_ref[...])
    sz_ref[...] = _silu(z).astype(sz_ref.dtype)

    tile_in_seq = i % (seq // tm)

    @pl.when(tile_in_seq == 0)
    def _():
        ext_ref[0:MAX_WINDOW, :] = jnp.zeros((MAX_WINDOW, tn), F32)

    @pl.when(tile_in_seq != 0)
    def _():
        ext_ref[0:MAX_WINDOW, :] = ext_ref[tm:tm + MAX_WINDOW, :]

    ext_ref[MAX_WINDOW:MAX_WINDOW + tm, :] = p
    pos = (tile_in_seq * tm + 1 + lax.broadcasted_iota(jnp.int32, (tm, 1), 0)).astype(F32)
    grp = j // (group_dim // tn)
    for gi, w in enumerate(POOL_WINDOWS):
        @pl.when(grp == gi)
        def _(w=w):
            s = p
            for dlt in range(1, w):
                s = s + ext_ref[MAX_WINDOW - dlt:MAX_WINDOW - dlt + tm, :]
            cnt = jnp.minimum(pos, float(w))
            pooled_ref[...] = (s / cnt - p).astype(pooled_ref.dtype)


def _b_in(h, w_in, e, seq, tm=1024, tn=512):
    m, d = h.shape
    nj = e // tn
    group_dim = e // len(POOL_WINDOWS)
    return pl.pallas_call(
        functools.partial(_b_in_kernel, seq=seq, group_dim=group_dim),
        grid=(nj, m // tm),
        in_specs=[pl.BlockSpec((tm, d), lambda j, i: (i, 0)),
                  pl.BlockSpec((d, tn), lambda j, i: (0, j)),
                  pl.BlockSpec((d, tn), lambda j, i: (0, nj + j))],
        out_specs=[pl.BlockSpec((tm, tn), lambda j, i: (i, j)),
                   pl.BlockSpec((tm, tn), lambda j, i: (i, j))],
        out_shape=[jax.ShapeDtypeStruct((m, e), BF16),
                   jax.ShapeDtypeStruct((m, e), BF16)],
        scratch_shapes=[pltpu.VMEM((MAX_WINDOW + tm, tn), F32)],
        compiler_params=_params(("arbitrary", "arbitrary"), 56),
        name="b_in",
    )(h, w_in, w_in)


def _b_out_kernel(pooled_ref, wg_ref, bg_ref, sc_ref, sz_ref, wo_ref, x_ref, pg_ref, o_ref):
    g = pl.program_id(1)
    n = pl.program_id(2)
    step = g * pl.num_programs(2) + n
    mixed = _dot(pooled_ref[...], wg_ref[0]) + bg_ref[...]
    y = (mixed * sc_ref[...] * sz_ref[...].astype(F32)).astype(BF16)
    _accumulate_out_projection(
        y, wo_ref, x_ref, pg_ref, o_ref, first=step == 0,
        last=step == pl.num_programs(1) * pl.num_programs(2) - 1)


def _b_out(pooled, w_grp, b_grp, scale, sz, w_out, x2d, post_gain, tm=512, tn=512):
    m, e = pooled.shape
    d = w_out.shape[1]
    groups, gw, _ = w_grp.shape
    nn = gw // tn
    row = lambda i, g, n: (i, 0)
    chan = lambda i, g, n: (0, g * nn + n)
    return pl.pallas_call(
        _b_out_kernel,
        grid=(m // tm, groups, nn),
        in_specs=[pl.BlockSpec((tm, gw), lambda i, g, n: (i, g)),
                  pl.BlockSpec((1, gw, tn), lambda i, g, n: (g, 0, n)),
                  pl.BlockSpec((1, tn), chan),
                  pl.BlockSpec((1, tn), chan),
                  pl.BlockSpec((tm, tn), lambda i, g, n: (i, g * nn + n)),
                  pl.BlockSpec((tn, d), lambda i, g, n: (g * nn + n, 0)),
                  pl.BlockSpec((tm, d), row, pipeline_mode=pl.Buffered(1)),
                  pl.BlockSpec((1, d), lambda i, g, n: (0, 0))],
        out_specs=pl.BlockSpec((tm, d), row),
        out_shape=jax.ShapeDtypeStruct((m, d), F32),
        compiler_params=_params(("parallel", "arbitrary", "arbitrary"), 56),
        name="b_out",
    )(pooled, w_grp, b_grp.reshape(1, e), scale.reshape(1, e), sz, w_out, x2d,
      post_gain.reshape(1, d))


def kernel(x, pre_norm, post_norm, a_w_in, a_ln_g, a_ln_b, a_w_s, a_b_s, a_w_out,
           b_w_in, b_w_grp, b_b_grp, b_scale, b_w_out):
    bsz, seq, d = x.shape
    e = a_w_out.shape[1]
    x2d = x.reshape(bsz * seq, d)

    h = _prenorm(x2d, pre_norm[0])
    a, g, s1, s2 = _a_in(h, a_w_in[0].astype(BF16), e)
    x2d = _a_out(a, g, s1, s2, a_ln_g[0], a_ln_b[0], a_w_s[0], a_b_s[0],
                 a_w_out[0].astype(BF16), x2d, post_norm[0])

    h = _prenorm(x2d, pre_norm[1])
    pooled, sz = _b_in(h, b_w_in[0].astype(BF16), e, seq)
    x2d = _b_out(pooled, b_w_grp[0].astype(BF16), b_b_grp[0], b_scale[0], sz,
                 b_w_out[0].astype(BF16), x2d, post_norm[1])
    return x2d.reshape(bsz, seq, d)
```

```python
import functools
import math

import jax
import jax.numpy as jnp
from jax import lax
from jax.experimental import pallas as pl
from jax.experimental.pallas import tpu as pltpu

EPS = 1e-6
CHUNK = 128
POOL_WINDOWS = (2, 4, 8, 16)
MAX_WINDOW = max(POOL_WINDOWS)
LANES = 128
MIB = 1024 * 1024

F32 = jnp.float32
BF16 = jnp.bfloat16


def _params(semantics, vmem_mib):
    return pltpu.CompilerParams(dimension_semantics=semantics,
                                vmem_limit_bytes=vmem_mib * MIB)


def _gelu(x):
    return 0.5 * x * (1.0 + lax.erf(x * math.sqrt(0.5)))


def _silu(x):
    return x / (1.0 + jnp.exp(-x))


def _dot(a, b):
    return jnp.dot(a, b, preferred_element_type=F32)


def _prenorm_kernel(x_ref, g_ref, h_ref):
    x = x_ref[...]
    ms = jnp.mean(x * x, axis=-1, keepdims=True)
    h_ref[...] = (x * lax.rsqrt(ms + EPS) * g_ref[...]).astype(h_ref.dtype)


def _prenorm(x2d, gain, tm=512):
    m, d = x2d.shape
    return pl.pallas_call(
        _prenorm_kernel,
        grid=(m // tm,),
        in_specs=[pl.BlockSpec((tm, d), lambda i: (i, 0)),
                  pl.BlockSpec((1, d), lambda i: (0, 0))],
        out_specs=pl.BlockSpec((tm, d), lambda i: (i, 0)),
        out_shape=jax.ShapeDtypeStruct((m, d), BF16),
        compiler_params=_params(("parallel",), 40),
        name="prenorm",
    )(x2d, gain.reshape(1, d))


def _a_in_kernel(h_ref, wu_ref, wv_ref, wz_ref, a_ref, g_ref, s1_ref, s2_ref):
    j = pl.program_id(1)
    h = h_ref[...]
    u = _dot(h, wu_ref[...])
    v = _dot(h, wv_ref[...])
    z = _dot(h, wz_ref[...])
    a_ref[...] = (_gelu(u) * _silu(z)).astype(a_ref.dtype)
    gv = _gelu(v)
    g_ref[...] = gv.astype(g_ref.dtype)
    tn = gv.shape[1]
    p1 = gv[:, 0:LANES]
    p2 = p1 * p1
    for k in range(1, tn // LANES):
        blk = gv[:, k * LANES:(k + 1) * LANES]
        p1 = p1 + blk
        p2 = p2 + blk * blk

    @pl.when(j == 0)
    def _():
        s1_ref[...] = p1
        s2_ref[...] = p2

    @pl.when(j > 0)
    def _():
        s1_ref[...] += p1
        s2_ref[...] += p2


def _a_in(h, w_in, e, tm=1024, tn=256):
    m, d = h.shape
    nj = e // tn
    row = lambda i, j: (i, 0)
    return pl.pallas_call(
        _a_in_kernel,
        grid=(m // tm, nj),
        in_specs=[pl.BlockSpec((tm, d), row),
                  pl.BlockSpec((d, tn), lambda i, j: (0, j)),
                  pl.BlockSpec((d, tn), lambda i, j: (0, nj + j)),
                  pl.BlockSpec((d, tn), lambda i, j: (0, 2 * nj + j))],
        out_specs=[pl.BlockSpec((tm, tn), lambda i, j: (i, j)),
                   pl.BlockSpec((tm, tn), lambda i, j: (i, j)),
                   pl.BlockSpec((tm, LANES), row),
                   pl.BlockSpec((tm, LANES), row)],
        out_shape=[jax.ShapeDtypeStruct((m, e), BF16),
                   jax.ShapeDtypeStruct((m, e), BF16),
                   jax.ShapeDtypeStruct((m, LANES), F32),
                   jax.ShapeDtypeStruct((m, LANES), F32)],
        compiler_params=_params(("parallel", "arbitrary"), 56),
        name="a_in",
    )(h, w_in, w_in, w_in)


OUT_COL_SPLIT = 4
EPILOGUE_ROWS = 64


def _out_projection_pipeline(step, n_make, make_y, y_buf, wo_ref, x_ref, pg_ref, o_ref):
    tm, d = o_ref.shape
    dn = d // OUT_COL_SPLIT
    slot = step % 2

    def project():
        y = y_buf[1 - slot]
        for nb in range(OUT_COL_SPLIT):
            cols = slice(nb * dn, (nb + 1) * dn)
            o_ref[:, cols] += _dot(y, wo_ref[:, cols])

    @pl.when(step == 0)
    def _():
        o_ref[...] = jnp.zeros(o_ref.shape, o_ref.dtype)
        make_y(y_buf.at[slot])

    @pl.when(jnp.logical_and(step > 0, step < n_make))
    def _():
        project()
        make_y(y_buf.at[slot])

    @pl.when(step == n_make)
    def _():
        project()

        def body(r, carry):
            rows = pl.ds(pl.multiple_of(r * EPILOGUE_ROWS, EPILOGUE_ROWS), EPILOGUE_ROWS)
            mm = o_ref[rows, :]
            ms = jnp.mean(mm * mm, axis=-1, keepdims=True)
            o_ref[rows, :] = x_ref[rows, :] + mm * lax.rsqrt(ms + EPS) * pg_ref[...]
            return carry
        lax.fori_loop(0, tm // EPILOGUE_ROWS, body, 0)


def _a_out_kernel(a_ref, g_ref, s1_ref, s2_ref, lng_ref, lnb_ref, ws_ref, bs_ref,
                  wo_ref, x_ref, pg_ref, o_ref, y_buf, *, e):
    k = pl.program_id(1)
    heads = pl.num_programs(1) - 1
    tm = a_ref.shape[0]

    def make_y(y_ref):
        mu = jnp.sum(s1_ref[...], axis=-1, keepdims=True) * (1.0 / e)
        var = jnp.sum(s2_ref[...], axis=-1, keepdims=True) * (1.0 / e) - mu * mu
        rstd = lax.rsqrt(var + EPS)
        g = g_ref[...].astype(F32)
        vn = ((g - mu) * rstd * lng_ref[...] + lnb_ref[...]).astype(BF16)
        w = ws_ref[0]
        t_idx = lax.broadcasted_iota(jnp.int32, w.shape, 0)
        s_idx = lax.broadcasted_iota(jnp.int32, w.shape, 1)
        wc = jnp.where(t_idx >= s_idx, w, 0.0).astype(BF16)
        bs = bs_ref[0]
        for c in range(tm // CHUNK):
            rows = slice(c * CHUNK, (c + 1) * CHUNK)
            sv = _dot(wc, vn[rows, :]) + bs
            y_ref[rows, :] = (a_ref[rows, :].astype(F32) * sv).astype(BF16)

    _out_projection_pipeline(k, heads, make_y, y_buf, wo_ref, x_ref, pg_ref, o_ref)


def _a_out(a, g, s1, s2, ln_g, ln_b, w_s, b_s, w_out, x2d, post_gain, tm=512):
    m, e = a.shape
    d = w_out.shape[1]
    heads = w_s.shape[0]
    hd = e // heads
    row = lambda i, k: (i, 0)
    make = lambda k: jnp.minimum(k, heads - 1)
    proj = lambda k: jnp.maximum(k - 1, 0)
    return pl.pallas_call(
        functools.partial(_a_out_kernel, e=e),
        grid=(m // tm, heads + 1),
        in_specs=[pl.BlockSpec((tm, hd), lambda i, k: (i, make(k))),
                  pl.BlockSpec((tm, hd), lambda i, k: (i, make(k))),
                  pl.BlockSpec((tm, LANES), row),
                  pl.BlockSpec((tm, LANES), row),
                  pl.BlockSpec((1, hd), lambda i, k: (0, make(k))),
                  pl.BlockSpec((1, hd), lambda i, k: (0, make(k))),
                  pl.BlockSpec((1, CHUNK, CHUNK), lambda i, k: (make(k), 0, 0)),
                  pl.BlockSpec((1, CHUNK, 1), lambda i, k: (make(k), 0, 0)),
                  pl.BlockSpec((hd, d), lambda i, k: (proj(k), 0)),
                  pl.BlockSpec((tm, d), row, pipeline_mode=pl.Buffered(1)),
                  pl.BlockSpec((1, d), lambda i, k: (0, 0))],
        out_specs=pl.BlockSpec((tm, d), row),
        out_shape=jax.ShapeDtypeStruct((m, d), F32),
        scratch_shapes=[pltpu.VMEM((2, tm, hd), BF16)],
        compiler_params=_params(("parallel", "arbitrary"), 56),
        name="a_out",
    )(a, g, s1, s2, ln_g.reshape(1, e), ln_b.reshape(1, e), w_s,
      b_s.reshape(heads, CHUNK, 1), w_out, x2d, post_gain.reshape(1, d))


def _b_in_kernel(h_ref, wp_ref, wz_ref, pooled_ref, sz_ref, ext_ref, *, seq, group_dim):
    j = pl.program_id(0)
    i = pl.program_id(1)
    tm, tn = pooled_ref.shape
    h = h_ref[...]
    p = _dot(h, wp_ref[...])
    z = _dot(h, wz_ref[...])
    sz_ref[...] = _silu(z).astype(sz_ref.dtype)

    tile_in_seq = i % (seq // tm)

    @pl.when(tile_in_seq == 0)
    def _():
        ext_ref[0:MAX_WINDOW, :] = jnp.zeros((MAX_WINDOW, tn), F32)

    @pl.when(tile_in_seq != 0)
    def _():
        ext_ref[0:MAX_WINDOW, :] = ext_ref[tm:tm + MAX_WINDOW, :]

    ext_ref[MAX_WINDOW:MAX_WINDOW + tm, :] = p
    pos = (tile_in_seq * tm + 1 + lax.broadcasted_iota(jnp.int32, (tm, 1), 0)).astype(F32)
    grp = j // (group_dim // tn)
    for gi, w in enumerate(POOL_WINDOWS):
        @pl.when(grp == gi)
        def _(w=w):
            s = p
            for dlt in range(1, w):
                s = s + ext_ref[MAX_WINDOW - dlt:MAX_WINDOW - dlt + tm, :]
            cnt = jnp.minimum(pos, float(w))
            pooled_ref[...] = (s / cnt - p).astype(pooled_ref.dtype)


def _b_in(h, w_in, e, seq, tm=1024, tn=512):
    m, d = h.shape
    nj = e // tn
    group_dim = e // len(POOL_WINDOWS)
    return pl.pallas_call(
        functools.partial(_b_in_kernel, seq=seq, group_dim=group_dim),
        grid=(nj, m // tm),
        in_specs=[pl.BlockSpec((tm, d), lambda j, i: (i, 0)),
                  pl.BlockSpec((d, tn), lambda j, i: (0, j)),
                  pl.BlockSpec((d, tn), lambda j, i: (0, nj + j))],
        out_specs=[pl.BlockSpec((tm, tn), lambda j, i: (i, j)),
                   pl.BlockSpec((tm, tn), lambda j, i: (i, j))],
        out_shape=[jax.ShapeDtypeStruct((m, e), BF16),
                   jax.ShapeDtypeStruct((m, e), BF16)],
        scratch_shapes=[pltpu.VMEM((MAX_WINDOW + tm, tn), F32)],
        compiler_params=_params(("arbitrary", "arbitrary"), 56),
        name="b_in",
    )(h, w_in, w_in)


def _b_out_kernel(pooled_ref, wg_ref, bg_ref, sc_ref, sz_ref, wo_ref, x_ref, pg_ref,
                  o_ref, y_buf):
    s = pl.program_id(1)
    n_blocks = pl.num_programs(1) - 1

    def make_y(y_ref):
        mixed = _dot(pooled_ref[...], wg_ref[0]) + bg_ref[...]
        y_ref[...] = (mixed * sc_ref[...] * sz_ref[...].astype(F32)).astype(BF16)

    _out_projection_pipeline(s, n_blocks, make_y, y_buf, wo_ref, x_ref, pg_ref, o_ref)


def _b_out(pooled, w_grp, b_grp, scale, sz, w_out, x2d, post_gain, tm=512, tn=512):
    m, e = pooled.shape
    d = w_out.shape[1]
    groups, gw, _ = w_grp.shape
    nn = gw // tn
    n_blocks = groups * nn
    row = lambda i, s: (i, 0)
    make = lambda s: jnp.minimum(s, n_blocks - 1)
    proj = lambda s: jnp.maximum(s - 1, 0)
    return pl.pallas_call(
        _b_out_kernel,
        grid=(m // tm, n_blocks + 1),
        in_specs=[pl.BlockSpec((tm, gw), lambda i, s: (i, make(s) // nn)),
                  pl.BlockSpec((1, gw, tn), lambda i, s: (make(s) // nn, 0, make(s) % nn)),
                  pl.BlockSpec((1, tn), lambda i, s: (0, make(s))),
                  pl.BlockSpec((1, tn), lambda i, s: (0, make(s))),
                  pl.BlockSpec((tm, tn), lambda i, s: (i, make(s))),
                  pl.BlockSpec((tn, d), lambda i, s: (proj(s), 0)),
                  pl.BlockSpec((tm, d), row, pipeline_mode=pl.Buffered(1)),
                  pl.BlockSpec((1, d), lambda i, s: (0, 0))],
        out_specs=pl.BlockSpec((tm, d), row),
        out_shape=jax.ShapeDtypeStruct((m, d), F32),
        scratch_shapes=[pltpu.VMEM((2, tm, tn), BF16)],
        compiler_params=_params(("parallel", "arbitrary"), 56),
        name="b_out",
    )(pooled, w_grp, b_grp.reshape(1, e), scale.reshape(1, e), sz, w_out, x2d,
      post_gain.reshape(1, d))


def kernel(x, pre_norm, post_norm, a_w_in, a_ln_g, a_ln_b, a_w_s, a_b_s, a_w_out,
           b_w_in, b_w_grp, b_b_grp, b_scale, b_w_out):
    bsz, seq, d = x.shape
    e = a_w_out.shape[1]
    x2d = x.reshape(bsz * seq, d)

    h = _prenorm(x2d, pre_norm[0])
    a, g, s1, s2 = _a_in(h, a_w_in[0].astype(BF16), e)
    x2d = _a_out(a, g, s1, s2, a_ln_g[0], a_ln_b[0], a_w_s[0], a_b_s[0],
                 a_w_out[0].astype(BF16), x2d, post_norm[0])

    h = _prenorm(x2d, pre_norm[1])
    pooled, sz = _b_in(h, b_w_in[0].astype(BF16), e, seq)
    x2d = _b_out(pooled, b_w_grp[0].astype(BF16), b_b_grp[0], b_scale[0], sz,
                 b_w_out[0].astype(BF16), x2d, post_norm[1])
    return x2d.reshape(bsz, seq, d)
```

```python
import functools
import math

import jax
import jax.numpy as jnp
from jax import lax
from jax.experimental import pallas as pl
from jax.experimental.pallas import tpu as pltpu

EPS = 1e-6
CHUNK = 128
POOL_WINDOWS = (2, 4, 8, 16)
MAX_WINDOW = max(POOL_WINDOWS)
LANES = 128
MIB = 1024 * 1024

F32 = jnp.float32
BF16 = jnp.bfloat16


def _params(semantics, vmem_mib):
    return pltpu.CompilerParams(dimension_semantics=semantics,
                                vmem_limit_bytes=vmem_mib * MIB)


def _gelu(x):
    return 0.5 * x * (1.0 + lax.erf(x * math.sqrt(0.5)))


def _silu(x):
    return x / (1.0 + jnp.exp(-x))


def _dot(a, b):
    return jnp.dot(a, b, preferred_element_type=F32)


def _prenorm_kernel(x_ref, g_ref, h_ref):
    x = x_ref[...]
    ms = jnp.mean(x * x, axis=-1, keepdims=True)
    h = (x * lax.rsqrt(ms + EPS) * g_ref[...]).astype(BF16)
    h_ref[...] = pltpu.bitcast(h, jnp.uint32)


def _prenorm(x2d, gain, tm=512):
    m, d = x2d.shape
    return pl.pallas_call(
        _prenorm_kernel,
        grid=(m // tm,),
        in_specs=[pl.BlockSpec((tm, d), lambda i: (i, 0)),
                  pl.BlockSpec((1, d), lambda i: (0, 0))],
        out_specs=pl.BlockSpec((tm // 2, d), lambda i: (i, 0)),
        out_shape=jax.ShapeDtypeStruct((m // 2, d), jnp.uint32),
        compiler_params=_params(("parallel",), 40),
        name="prenorm",
    )(x2d, gain.reshape(1, d))


def _a_in_kernel(h_ref, wu_ref, wv_ref, wz_ref, a_ref, g_ref, s1_ref, s2_ref):
    j = pl.program_id(1)
    h = pltpu.bitcast(h_ref[...], BF16)
    u = _dot(h, wu_ref[...])
    v = _dot(h, wv_ref[...])
    z = _dot(h, wz_ref[...])
    a_ref[...] = (_gelu(u) * _silu(z)).astype(a_ref.dtype)
    gv = _gelu(v)
    g_ref[...] = gv.astype(g_ref.dtype)
    tn = gv.shape[1]
    p1 = gv[:, 0:LANES]
    p2 = p1 * p1
    for k in range(1, tn // LANES):
        blk = gv[:, k * LANES:(k + 1) * LANES]
        p1 = p1 + blk
        p2 = p2 + blk * blk

    @pl.when(j == 0)
    def _():
        s1_ref[...] = p1
        s2_ref[...] = p2

    @pl.when(j > 0)
    def _():
        s1_ref[...] += p1
        s2_ref[...] += p2


def _a_in(h, w_in, e, tm=1024, tn=256):
    m, d = 2 * h.shape[0], h.shape[1]
    nj = e // tn
    row = lambda i, j: (i, 0)
    return pl.pallas_call(
        _a_in_kernel,
        grid=(m // tm, nj),
        in_specs=[pl.BlockSpec((tm // 2, d), row),
                  pl.BlockSpec((d, tn), lambda i, j: (0, j)),
                  pl.BlockSpec((d, tn), lambda i, j: (0, nj + j)),
                  pl.BlockSpec((d, tn), lambda i, j: (0, 2 * nj + j))],
        out_specs=[pl.BlockSpec((tm, tn), lambda i, j: (i, j)),
                   pl.BlockSpec((tm, tn), lambda i, j: (i, j)),
                   pl.BlockSpec((tm, LANES), row),
                   pl.BlockSpec((tm, LANES), row)],
        out_shape=[jax.ShapeDtypeStruct((m, e), BF16),
                   jax.ShapeDtypeStruct((m, e), BF16),
                   jax.ShapeDtypeStruct((m, LANES), F32),
                   jax.ShapeDtypeStruct((m, LANES), F32)],
        compiler_params=_params(("parallel", "arbitrary"), 56),
        name="a_in",
    )(h, w_in, w_in, w_in)


OUT_COL_SPLIT = 4
EPILOGUE_ROWS = 64


def _out_projection_pipeline(step, n_make, make_y, y_buf, wo_ref, x_ref, pg_ref, o_ref):
    tm, d = o_ref.shape
    dn = d // OUT_COL_SPLIT
    slot = step % 2

    def project():
        y = y_buf[1 - slot]
        for nb in range(OUT_COL_SPLIT):
            cols = slice(nb * dn, (nb + 1) * dn)
            o_ref[:, cols] += _dot(y, wo_ref[:, cols])

    @pl.when(step == 0)
    def _():
        o_ref[...] = jnp.zeros(o_ref.shape, o_ref.dtype)
        make_y(y_buf.at[slot])

    @pl.when(jnp.logical_and(step > 0, step < n_make))
    def _():
        project()
        make_y(y_buf.at[slot])

    @pl.when(step == n_make)
    def _():
        project()

        def body(r, carry):
            rows = pl.ds(pl.multiple_of(r * EPILOGUE_ROWS, EPILOGUE_ROWS), EPILOGUE_ROWS)
            mm = o_ref[rows, :]
            ms = jnp.mean(mm * mm, axis=-1, keepdims=True)
            o_ref[rows, :] = x_ref[rows, :] + mm * lax.rsqrt(ms + EPS) * pg_ref[...]
            return carry
        lax.fori_loop(0, tm // EPILOGUE_ROWS, body, 0)


def _a_out_kernel(a_ref, g_ref, s1_ref, s2_ref, lng_ref, lnb_ref, ws_ref, bs_ref,
                  wo_ref, x_ref, pg_ref, o_ref, y_buf, *, e):
    k = pl.program_id(1)
    heads = pl.num_programs(1) - 1
    tm = a_ref.shape[0]

    def make_y(y_ref):
        mu = jnp.sum(s1_ref[...], axis=-1, keepdims=True) * (1.0 / e)
        var = jnp.sum(s2_ref[...], axis=-1, keepdims=True) * (1.0 / e) - mu * mu
        rstd = lax.rsqrt(var + EPS)
        g = g_ref[...].astype(F32)
        vn = ((g - mu) * rstd * lng_ref[...] + lnb_ref[...]).astype(BF16)
        w = ws_ref[0]
        t_idx = lax.broadcasted_iota(jnp.int32, w.shape, 0)
        s_idx = lax.broadcasted_iota(jnp.int32, w.shape, 1)
        wc = jnp.where(t_idx >= s_idx, w, 0.0).astype(BF16)
        bs = bs_ref[0]
        for c in range(tm // CHUNK):
            rows = slice(c * CHUNK, (c + 1) * CHUNK)
            sv = _dot(wc, vn[rows, :]) + bs
            y_ref[rows, :] = (a_ref[rows, :].astype(F32) * sv).astype(BF16)

    _out_projection_pipeline(k, heads, make_y, y_buf, wo_ref, x_ref, pg_ref, o_ref)


def _a_out(a, g, s1, s2, ln_g, ln_b, w_s, b_s, w_out, x2d, post_gain, tm=512):
    m, e = a.shape
    d = w_out.shape[1]
    heads = w_s.shape[0]
    hd = e // heads
    row = lambda i, k: (i, 0)
    make = lambda k: jnp.minimum(k, heads - 1)
    proj = lambda k: jnp.maximum(k - 1, 0)
    return pl.pallas_call(
        functools.partial(_a_out_kernel, e=e),
        grid=(m // tm, heads + 1),
        in_specs=[pl.BlockSpec((tm, hd), lambda i, k: (i, make(k))),
                  pl.BlockSpec((tm, hd), lambda i, k: (i, make(k))),
                  pl.BlockSpec((tm, LANES), row),
                  pl.BlockSpec((tm, LANES), row),
                  pl.BlockSpec((1, hd), lambda i, k: (0, make(k))),
                  pl.BlockSpec((1, hd), lambda i, k: (0, make(k))),
                  pl.BlockSpec((1, CHUNK, CHUNK), lambda i, k: (make(k), 0, 0)),
                  pl.BlockSpec((1, CHUNK, 1), lambda i, k: (make(k), 0, 0)),
                  pl.BlockSpec((hd, d), lambda i, k: (proj(k), 0)),
                  pl.BlockSpec((tm, d), row),
                  pl.BlockSpec((1, d), lambda i, k: (0, 0))],
        out_specs=pl.BlockSpec((tm, d), row),
        out_shape=jax.ShapeDtypeStruct((m, d), F32),
        scratch_shapes=[pltpu.VMEM((2, tm, hd), BF16)],
        compiler_params=_params(("parallel", "arbitrary"), 56),
        name="a_out",
    )(a, g, s1, s2, ln_g.reshape(1, e), ln_b.reshape(1, e), w_s,
      b_s.reshape(heads, CHUNK, 1), w_out, x2d, post_gain.reshape(1, d))


def _trailing_window_sums(ext, grp):
    s = ext
    sel = None
    span = 1
    for gi, w in enumerate(POOL_WINDOWS):
        while span < w:
            s = s + pltpu.roll(s, span, 0)
            span *= 2
        assert span == w, "pool windows must be ascending powers of two"
        sel = s if sel is None else jnp.where(grp == gi, s, sel)
    return sel


def _b_in_kernel(h_ref, wp_ref, wz_ref, pooled_ref, sz_ref, wbf_ref, carry_ref,
                 *, seq, group_dim):
    j = pl.program_id(0)
    i = pl.program_id(1)
    tm, tn = pooled_ref.shape

    @pl.when(i == 0)
    def _():
        wbf_ref[0] = wp_ref[...].astype(BF16)
        wbf_ref[1] = wz_ref[...].astype(BF16)
        carry_ref[...] = jnp.zeros(carry_ref.shape, F32)

    h = pltpu.bitcast(h_ref[...], BF16)
    p = _dot(h, wbf_ref[0])
    z = _dot(h, wbf_ref[1])
    sz_ref[...] = _silu(z).astype(sz_ref.dtype)
    tile_in_seq = i % (seq // tm)
    halo = jnp.where(tile_in_seq == 0, 0.0, carry_ref[...])
    carry_ref[...] = p[tm - MAX_WINDOW:, :]
    grp = j // (group_dim // tn)
    sums = _trailing_window_sums(jnp.concatenate([halo, p], axis=0), grp)[MAX_WINDOW:, :]
    pos = tile_in_seq * tm + 1 + lax.broadcasted_iota(jnp.int32, (tm, 1), 0)
    window = jnp.left_shift(POOL_WINDOWS[0], grp)
    inv_cnt = 1.0 / jnp.minimum(pos, window).astype(F32)
    pooled_ref[...] = (sums * inv_cnt - p).astype(pooled_ref.dtype)


def _b_in(h, w_in, e, seq, tm=1024, tn=256):
    m, d = 2 * h.shape[0], h.shape[1]
    nj = e // tn
    group_dim = e // len(POOL_WINDOWS)
    assert all(w == POOL_WINDOWS[0] << k for k, w in enumerate(POOL_WINDOWS))
    return pl.pallas_call(
        functools.partial(_b_in_kernel, seq=seq, group_dim=group_dim),
        grid=(nj, m // tm),
        in_specs=[pl.BlockSpec((tm // 2, d), lambda j, i: (i, 0)),
                  pl.BlockSpec((d, tn), lambda j, i: (0, j)),
                  pl.BlockSpec((d, tn), lambda j, i: (0, nj + j))],
        out_specs=[pl.BlockSpec((tm, tn), lambda j, i: (i, j)),
                   pl.BlockSpec((tm, tn), lambda j, i: (i, j))],
        out_shape=[jax.ShapeDtypeStruct((m, e), BF16),
                   jax.ShapeDtypeStruct((m, e), BF16)],
        scratch_shapes=[pltpu.VMEM((2, d, tn), BF16),
                        pltpu.VMEM((MAX_WINDOW, tn), F32)],
        compiler_params=_params(("arbitrary", "arbitrary"), 56),
        name="b_in",
    )(h, w_in, w_in)


def _b_out_kernel(pooled_ref, wg_ref, bg_ref, sc_ref, sz_ref, wo_ref, x_ref, pg_ref,
                  o_ref, y_buf):
    s = pl.program_id(1)
    n_blocks = pl.num_programs(1) - 1

    def make_y(y_ref):
        mixed = _dot(pooled_ref[...], wg_ref[0]) + bg_ref[...]
        y_ref[...] = (mixed * sc_ref[...] * sz_ref[...].astype(F32)).astype(BF16)

    _out_projection_pipeline(s, n_blocks, make_y, y_buf, wo_ref, x_ref, pg_ref, o_ref)


def _b_out(pooled, w_grp, b_grp, scale, sz, w_out, x2d, post_gain, tm=512, tn=512):
    m, e = pooled.shape
    d = w_out.shape[1]
    groups, gw, _ = w_grp.shape
    nn = gw // tn
    n_blocks = groups * nn
    row = lambda i, s: (i, 0)
    make = lambda s: jnp.minimum(s, n_blocks - 1)
    proj = lambda s: jnp.maximum(s - 1, 0)
    return pl.pallas_call(
        _b_out_kernel,
        grid=(m // tm, n_blocks + 1),
        in_specs=[pl.BlockSpec((tm, gw), lambda i, s: (i, make(s) // nn)),
                  pl.BlockSpec((1, gw, tn), lambda i, s: (make(s) // nn, 0, make(s) % nn)),
                  pl.BlockSpec((1, tn), lambda i, s: (0, make(s))),
                  pl.BlockSpec((1, tn), lambda i, s: (0, make(s))),
                  pl.BlockSpec((tm, tn), lambda i, s: (i, make(s))),
                  pl.BlockSpec((tn, d), lambda i, s: (proj(s), 0)),
                  pl.BlockSpec((tm, d), row),
                  pl.BlockSpec((1, d), lambda i, s: (0, 0))],
        out_specs=pl.BlockSpec((tm, d), row),
        out_shape=jax.ShapeDtypeStruct((m, d), F32),
        scratch_shapes=[pltpu.VMEM((2, tm, tn), BF16)],
        compiler_params=_params(("parallel", "arbitrary"), 56),
        name="b_out",
    )(pooled, w_grp, b_grp.reshape(1, e), scale.reshape(1, e), sz, w_out, x2d,
      post_gain.reshape(1, d))


def kernel(x, pre_norm, post_norm, a_w_in, a_ln_g, a_ln_b, a_w_s, a_b_s, a_w_out,
           b_w_in, b_w_grp, b_b_grp, b_scale, b_w_out):
    bsz, seq, d = x.shape
    e = a_w_out.shape[1]
    x2d = x.reshape(bsz * seq, d)

    h = _prenorm(x2d, pre_norm[0])
    a, g, s1, s2 = _a_in(h, a_w_in[0].astype(BF16), e)
    x2d = _a_out(a, g, s1, s2, a_ln_g[0], a_ln_b[0], a_w_s[0], a_b_s[0],
                 a_w_out[0].astype(BF16), x2d, post_norm[0])

    h = _prenorm(x2d, pre_norm[1])
    pooled, sz = _b_in(h, b_w_in[0], e, seq)
    x2d = _b_out(pooled, b_w_grp[0].astype(BF16), b_b_grp[0], b_scale[0], sz,
                 b_w_out[0].astype(BF16), x2d, post_norm[1])
    return x2d.reshape(bsz, seq, d)
```

```python
import functools
import math

import jax
import jax.numpy as jnp
from jax import lax
from jax.experimental import pallas as pl
from jax.experimental.pallas import tpu as pltpu

EPS = 1e-6
CHUNK = 128
POOL_WINDOWS = (2, 4, 8, 16)
MAX_WINDOW = max(POOL_WINDOWS)
LANES = 128
MIB = 1024 * 1024

F32 = jnp.float32
BF16 = jnp.bfloat16


def _params(semantics, vmem_mib):
    return pltpu.CompilerParams(dimension_semantics=semantics,
                                vmem_limit_bytes=vmem_mib * MIB)


def _gelu(x):
    return 0.5 * x * (1.0 + lax.erf(x * math.sqrt(0.5)))


def _silu(x):
    return x / (1.0 + jnp.exp(-x))


def _dot(a, b):
    return jnp.dot(a, b, preferred_element_type=F32)


def _prenorm_kernel(x_ref, g_ref, h_ref):
    x = x_ref[...]
    ms = jnp.mean(x * x, axis=-1, keepdims=True)
    h = (x * lax.rsqrt(ms + EPS) * g_ref[...]).astype(BF16)
    h_ref[...] = pltpu.bitcast(h, jnp.uint32)


def _prenorm(x2d, gain, tm=512):
    m, d = x2d.shape
    return pl.pallas_call(
        _prenorm_kernel,
        grid=(m // tm,),
        in_specs=[pl.BlockSpec((tm, d), lambda i: (i, 0)),
                  pl.BlockSpec((1, d), lambda i: (0, 0))],
        out_specs=pl.BlockSpec((tm // 2, d), lambda i: (i, 0)),
        out_shape=jax.ShapeDtypeStruct((m // 2, d), jnp.uint32),
        compiler_params=_params(("parallel",), 40),
        name="prenorm",
    )(x2d, gain.reshape(1, d))


def _a_in_kernel(h_ref, wu_ref, wv_ref, wz_ref, wof_ref, a_ref, g_ref, s1_ref, s2_ref,
                 wob_ref):
    j = pl.program_id(1)
    wob_ref[...] = wof_ref[...].astype(BF16)
    h = pltpu.bitcast(h_ref[...], BF16)
    u = _dot(h, wu_ref[...])
    v = _dot(h, wv_ref[...])
    z = _dot(h, wz_ref[...])
    a_ref[...] = (_gelu(u) * _silu(z)).astype(a_ref.dtype)
    gv = _gelu(v)
    g_ref[...] = gv.astype(g_ref.dtype)
    tn = gv.shape[1]
    p1 = gv[:, 0:LANES]
    p2 = p1 * p1
    for k in range(1, tn // LANES):
        blk = gv[:, k * LANES:(k + 1) * LANES]
        p1 = p1 + blk
        p2 = p2 + blk * blk

    @pl.when(j == 0)
    def _():
        s1_ref[...] = p1
        s2_ref[...] = p2

    @pl.when(j > 0)
    def _():
        s1_ref[...] += p1
        s2_ref[...] += p2


def _a_in(h, w_in, w_out, e, tm=1024, tn=256):
    m, d = 2 * h.shape[0], h.shape[1]
    nj = e // tn
    ni = m // tm
    slab_o = w_out.shape[0] // (ni * nj)
    row = lambda i, j: (i, 0)
    step = lambda i, j: (i * nj + j, 0)
    return pl.pallas_call(
        _a_in_kernel,
        grid=(ni, nj),
        in_specs=[pl.BlockSpec((tm // 2, d), row),
                  pl.BlockSpec((d, tn), lambda i, j: (0, j)),
                  pl.BlockSpec((d, tn), lambda i, j: (0, nj + j)),
                  pl.BlockSpec((d, tn), lambda i, j: (0, 2 * nj + j)),
                  pl.BlockSpec((slab_o, w_out.shape[1]), step)],
        out_specs=[pl.BlockSpec((tm, tn), lambda i, j: (i, j)),
                   pl.BlockSpec((tm, tn), lambda i, j: (i, j)),
                   pl.BlockSpec((tm, LANES), row),
                   pl.BlockSpec((tm, LANES), row),
                   pl.BlockSpec((slab_o, w_out.shape[1]), step)],
        out_shape=[jax.ShapeDtypeStruct((m, e), BF16),
                   jax.ShapeDtypeStruct((m, e), BF16),
                   jax.ShapeDtypeStruct((m, LANES), F32),
                   jax.ShapeDtypeStruct((m, LANES), F32),
                   jax.ShapeDtypeStruct(w_out.shape, BF16)],
        compiler_params=_params(("parallel", "arbitrary"), 56),
        name="a_in",
    )(h, w_in, w_in, w_in, w_out)


OUT_COL_SPLIT = 4
EPILOGUE_ROWS = 128


def _out_projection_pipeline(step, n_make, make_y, y_buf, wo_ref, x_ref, pg_ref, o_ref):
    tm, d = o_ref.shape
    dn = d // OUT_COL_SPLIT
    slot = step % 2

    def project():
        y = y_buf[1 - slot]
        for nb in range(OUT_COL_SPLIT):
            cols = slice(nb * dn, (nb + 1) * dn)
            o_ref[:, cols] += _dot(y, wo_ref[:, cols])

    @pl.when(step == 0)
    def _():
        o_ref[...] = jnp.zeros(o_ref.shape, o_ref.dtype)
        make_y(y_buf.at[slot])

    @pl.when(jnp.logical_and(step > 0, step < n_make))
    def _():
        project()
        make_y(y_buf.at[slot])

    @pl.when(step == n_make)
    def _():
        project()

        def body(r, carry):
            rows = pl.ds(pl.multiple_of(r * EPILOGUE_ROWS, EPILOGUE_ROWS), EPILOGUE_ROWS)
            mm = o_ref[rows, :]
            ms = jnp.mean(mm * mm, axis=-1, keepdims=True)
            o_ref[rows, :] = x_ref[rows, :] + mm * lax.rsqrt(ms + EPS) * pg_ref[...]
            return carry
        lax.fori_loop(0, tm // EPILOGUE_ROWS, body, 0)


def _a_out_kernel(a_ref, g_ref, s1_ref, s2_ref, lng_ref, lnb_ref, ws_ref, bs_ref,
                  wo_ref, x_ref, pg_ref, o_ref, y_buf, *, e):
    k = pl.program_id(1)
    heads = pl.num_programs(1) - 1
    tm = a_ref.shape[0]

    def make_y(y_ref):
        mu = jnp.sum(s1_ref[...], axis=-1, keepdims=True) * (1.0 / e)
        var = jnp.sum(s2_ref[...], axis=-1, keepdims=True) * (1.0 / e) - mu * mu
        rstd = lax.rsqrt(var + EPS)
        g = g_ref[...].astype(F32)
        vn = ((g - mu) * rstd * lng_ref[...] + lnb_ref[...]).astype(BF16)
        w = ws_ref[0]
        t_idx = lax.broadcasted_iota(jnp.int32, w.shape, 0)
        s_idx = lax.broadcasted_iota(jnp.int32, w.shape, 1)
        wc = jnp.where(t_idx >= s_idx, w, 0.0).astype(BF16)
        bs = bs_ref[0]
        for c in range(tm // CHUNK):
            rows = slice(c * CHUNK, (c + 1) * CHUNK)
            sv = _dot(wc, vn[rows, :]) + bs
            y_ref[rows, :] = (a_ref[rows, :].astype(F32) * sv).astype(BF16)

    _out_projection_pipeline(k, heads, make_y, y_buf, wo_ref, x_ref, pg_ref, o_ref)


def _a_out(a, g, s1, s2, ln_g, ln_b, w_s, b_s, w_out, x2d, post_gain, tm=512):
    m, e = a.shape
    d = w_out.shape[1]
    heads = w_s.shape[0]
    hd = e // heads
    row = lambda i, k: (i, 0)
    make = lambda k: jnp.minimum(k, heads - 1)
    proj = lambda k: jnp.maximum(k - 1, 0)
    return pl.pallas_call(
        functools.partial(_a_out_kernel, e=e),
        grid=(m // tm, heads + 1),
        in_specs=[pl.BlockSpec((tm, hd), lambda i, k: (i, make(k))),
                  pl.BlockSpec((tm, hd), lambda i, k: (i, make(k))),
                  pl.BlockSpec((tm, LANES), row),
                  pl.BlockSpec((tm, LANES), row),
                  pl.BlockSpec((1, hd), lambda i, k: (0, make(k))),
                  pl.BlockSpec((1, hd), lambda i, k: (0, make(k))),
                  pl.BlockSpec((1, CHUNK, CHUNK), lambda i, k: (make(k), 0, 0)),
                  pl.BlockSpec((1, CHUNK, 1), lambda i, k: (make(k), 0, 0)),
                  pl.BlockSpec((hd, d), lambda i, k: (proj(k), 0)),
                  pl.BlockSpec((tm, d), row),
                  pl.BlockSpec((1, d), lambda i, k: (0, 0))],
        out_specs=pl.BlockSpec((tm, d), row),
        out_shape=jax.ShapeDtypeStruct((m, d), F32),
        scratch_shapes=[pltpu.VMEM((2, tm, hd), BF16)],
        compiler_params=_params(("parallel", "arbitrary"), 56),
        name="a_out",
    )(a, g, s1, s2, ln_g.reshape(1, e), ln_b.reshape(1, e), w_s,
      b_s.reshape(heads, CHUNK, 1), w_out, x2d, post_gain.reshape(1, d))


def _trailing_window_sums(ext, window):
    s = ext
    span = 1
    while span < window:
        s = s + pltpu.roll(s, span, 0)
        span *= 2
    assert span == window, "pool windows must be powers of two"
    return s


def _b_in_kernel(h_ref, wp_ref, wz_ref, wgf_ref, wof_ref, pooled_ref, sz_ref, wgb_ref,
                 wob_ref, wbf_ref, carry_ref, *, seq, group_dim):
    j = pl.program_id(0)
    i = pl.program_id(1)
    tm, tn = pooled_ref.shape
    wgb_ref[...] = wgf_ref[...].astype(BF16)
    wob_ref[...] = wof_ref[...].astype(BF16)

    @pl.when(i == 0)
    def _():
        wbf_ref[0] = wp_ref[...].astype(BF16)
        wbf_ref[1] = wz_ref[...].astype(BF16)
        carry_ref[...] = jnp.zeros(carry_ref.shape, F32)

    h = pltpu.bitcast(h_ref[...], BF16)
    p = _dot(h, wbf_ref[0])
    z = _dot(h, wbf_ref[1])
    sz_ref[...] = _silu(z).astype(sz_ref.dtype)
    tile_in_seq = i % (seq // tm)
    halo = jnp.where(tile_in_seq == 0, 0.0, carry_ref[...])
    carry_ref[...] = p[tm - MAX_WINDOW:, :]
    ext = jnp.concatenate([halo, p], axis=0)
    head = slice(0, MAX_WINDOW)
    pos_head = tile_in_seq * tm + 1 + lax.broadcasted_iota(jnp.int32, (MAX_WINDOW, 1), 0)
    grp = j // (group_dim // tn)
    for gi, window in enumerate(POOL_WINDOWS):
        @pl.when(grp == gi)
        def _(window=window):
            sums = _trailing_window_sums(ext, window)[MAX_WINDOW:, :]
            pooled_ref[...] = (sums * (1.0 / window) - p).astype(pooled_ref.dtype)
            inv_cnt = 1.0 / jnp.minimum(pos_head, window).astype(F32)
            pooled_ref[head, :] = (sums[head, :] * inv_cnt - p[head, :]).astype(pooled_ref.dtype)


def _b_in(h, w_in, w_grp, w_out, e, seq, tm=1024, tn=256):
    m, d = 2 * h.shape[0], h.shape[1]
    nj = e // tn
    ni = m // tm
    group_dim = e // len(POOL_WINDOWS)
    wg2d = w_grp.reshape(-1, w_grp.shape[-1])
    slab_g = wg2d.shape[0] // (nj * ni)
    slab_o = w_out.shape[0] // (nj * ni)
    step = lambda j, i: (j * ni + i, 0)
    return pl.pallas_call(
        functools.partial(_b_in_kernel, seq=seq, group_dim=group_dim),
        grid=(nj, ni),
        in_specs=[pl.BlockSpec((tm // 2, d), lambda j, i: (i, 0)),
                  pl.BlockSpec((d, tn), lambda j, i: (0, j)),
                  pl.BlockSpec((d, tn), lambda j, i: (0, nj + j)),
                  pl.BlockSpec((slab_g, wg2d.shape[1]), step),
                  pl.BlockSpec((slab_o, w_out.shape[1]), step)],
        out_specs=[pl.BlockSpec((tm, tn), lambda j, i: (i, j)),
                   pl.BlockSpec((tm, tn), lambda j, i: (i, j)),
                   pl.BlockSpec((slab_g, wg2d.shape[1]), step),
                   pl.BlockSpec((slab_o, w_out.shape[1]), step)],
        out_shape=[jax.ShapeDtypeStruct((m, e), BF16),
                   jax.ShapeDtypeStruct((m, e), BF16),
                   jax.ShapeDtypeStruct(wg2d.shape, BF16),
                   jax.ShapeDtypeStruct(w_out.shape, BF16)],
        scratch_shapes=[pltpu.VMEM((2, d, tn), BF16),
                        pltpu.VMEM((MAX_WINDOW, tn), F32)],
        compiler_params=_params(("arbitrary", "arbitrary"), 56),
        name="b_in",
    )(h, w_in, w_in, wg2d, w_out)


def _b_out_kernel(pooled_ref, wg_ref, bg_ref, sc_ref, sz_ref, wo_ref, x_ref, pg_ref,
                  o_ref, y_buf):
    s = pl.program_id(1)
    n_blocks = pl.num_programs(1) - 1

    def make_y(y_ref):
        mixed = _dot(pooled_ref[...], wg_ref[0]) + bg_ref[...]
        y_ref[...] = (mixed * sc_ref[...] * sz_ref[...].astype(F32)).astype(BF16)

    _out_projection_pipeline(s, n_blocks, make_y, y_buf, wo_ref, x_ref, pg_ref, o_ref)


def _b_out(pooled, w_grp, b_grp, scale, sz, w_out, x2d, post_gain, tm=512, tn=512):
    m, e = pooled.shape
    d = w_out.shape[1]
    groups, gw, _ = w_grp.shape
    nn = gw // tn
    n_blocks = groups * nn
    row = lambda i, s: (i, 0)
    make = lambda s: jnp.minimum(s, n_blocks - 1)
    proj = lambda s: jnp.maximum(s - 1, 0)
    return pl.pallas_call(
        _b_out_kernel,
        grid=(m // tm, n_blocks + 1),
        in_specs=[pl.BlockSpec((tm, gw), lambda i, s: (i, make(s) // nn)),
                  pl.BlockSpec((1, gw, tn), lambda i, s: (make(s) // nn, 0, make(s) % nn)),
                  pl.BlockSpec((1, tn), lambda i, s: (0, make(s))),
                  pl.BlockSpec((1, tn), lambda i, s: (0, make(s))),
                  pl.BlockSpec((tm, tn), lambda i, s: (i, make(s))),
                  pl.BlockSpec((tn, d), lambda i, s: (proj(s), 0)),
                  pl.BlockSpec((tm, d), row),
                  pl.BlockSpec((1, d), lambda i, s: (0, 0))],
        out_specs=pl.BlockSpec((tm, d), row),
        out_shape=jax.ShapeDtypeStruct((m, d), F32),
        scratch_shapes=[pltpu.VMEM((2, tm, tn), BF16)],
        compiler_params=_params(("parallel", "arbitrary"), 56),
        name="b_out",
    )(pooled, w_grp, b_grp.reshape(1, e), scale.reshape(1, e), sz, w_out, x2d,
      post_gain.reshape(1, d))


def kernel(x, pre_norm, post_norm, a_w_in, a_ln_g, a_ln_b, a_w_s, a_b_s, a_w_out,
           b_w_in, b_w_grp, b_b_grp, b_scale, b_w_out):
    bsz, seq, d = x.shape
    e = a_w_out.shape[1]
    x2d = x.reshape(bsz * seq, d)

    h = _prenorm(x2d, pre_norm[0])
    a, g, s1, s2, a_wo = _a_in(h, a_w_in[0].astype(BF16), a_w_out[0], e)
    x2d = _a_out(a, g, s1, s2, a_ln_g[0], a_ln_b[0], a_w_s[0], a_b_s[0],
                 a_wo, x2d, post_norm[0])

    h = _prenorm(x2d, pre_norm[1])
    pooled, sz, b_wg, b_wo = _b_in(h, b_w_in[0], b_w_grp[0], b_w_out[0], e, seq)
    x2d = _b_out(pooled, b_wg.reshape(b_w_grp[0].shape), b_b_grp[0], b_scale[0], sz,
                 b_wo, x2d, post_norm[1])
    return x2d.reshape(bsz, seq, d)
```

```python
import functools
import math

import jax
import jax.numpy as jnp
from jax import lax
from jax.experimental import pallas as pl
from jax.experimental.pallas import tpu as pltpu

EPS = 1e-6
CHUNK = 128
POOL_WINDOWS = (2, 4, 8, 16)
MAX_WINDOW = max(POOL_WINDOWS)
LANES = 128
MIB = 1024 * 1024

F32 = jnp.float32
BF16 = jnp.bfloat16


def _params(semantics, vmem_mib):
    return pltpu.CompilerParams(dimension_semantics=semantics,
                                vmem_limit_bytes=vmem_mib * MIB)


def _gelu(x):
    return 0.5 * x * (1.0 + lax.erf(x * math.sqrt(0.5)))


def _silu(x):
    return x / (1.0 + jnp.exp(-x))


def _dot(a, b):
    return jnp.dot(a, b, preferred_element_type=F32)


def _prenorm_kernel(x_ref, g_ref, h_ref):
    x = x_ref[...]
    ms = jnp.mean(x * x, axis=-1, keepdims=True)
    h = (x * lax.rsqrt(ms + EPS) * g_ref[...]).astype(BF16)
    h_ref[...] = pltpu.bitcast(h, jnp.uint32)


def _prenorm(x2d, gain, tm=512):
    m, d = x2d.shape
    return pl.pallas_call(
        _prenorm_kernel,
        grid=(m // tm,),
        in_specs=[pl.BlockSpec((tm, d), lambda i: (i, 0)),
                  pl.BlockSpec((1, d), lambda i: (0, 0))],
        out_specs=pl.BlockSpec((tm // 2, d), lambda i: (i, 0)),
        out_shape=jax.ShapeDtypeStruct((m // 2, d), jnp.uint32),
        compiler_params=_params(("parallel",), 40),
        name="prenorm",
    )(x2d, gain.reshape(1, d))


def _a_in_kernel(h_ref, wu_ref, wv_ref, wz_ref, wof_ref, a_ref, g_ref, s1_ref, s2_ref,
                 wob_ref):
    j = pl.program_id(1)
    wob_ref[...] = wof_ref[...].astype(BF16)
    h = pltpu.bitcast(h_ref[...], BF16)
    u = _dot(h, wu_ref[...].astype(BF16))
    v = _dot(h, wv_ref[...].astype(BF16))
    z = _dot(h, wz_ref[...].astype(BF16))
    a_ref[...] = (_gelu(u) * _silu(z)).astype(a_ref.dtype)
    gv = _gelu(v)
    g_ref[...] = gv.astype(g_ref.dtype)
    tn = gv.shape[1]
    p1 = gv[:, 0:LANES]
    p2 = p1 * p1
    for k in range(1, tn // LANES):
        blk = gv[:, k * LANES:(k + 1) * LANES]
        p1 = p1 + blk
        p2 = p2 + blk * blk

    @pl.when(j == 0)
    def _():
        s1_ref[...] = p1
        s2_ref[...] = p2

    @pl.when(j > 0)
    def _():
        s1_ref[...] += p1
        s2_ref[...] += p2


def _a_in(h, w_in, w_out, e, tm=1024, tn=256):
    m, d = 2 * h.shape[0], h.shape[1]
    nj = e // tn
    ni = m // tm
    slab_o = w_out.shape[0] // (ni * nj)
    row = lambda i, j: (i, 0)
    step = lambda i, j: (i * nj + j, 0)
    return pl.pallas_call(
        _a_in_kernel,
        grid=(ni, nj),
        in_specs=[pl.BlockSpec((tm // 2, d), row),
                  pl.BlockSpec((d, tn), lambda i, j: (0, j)),
                  pl.BlockSpec((d, tn), lambda i, j: (0, nj + j)),
                  pl.BlockSpec((d, tn), lambda i, j: (0, 2 * nj + j)),
                  pl.BlockSpec((slab_o, w_out.shape[1]), step)],
        out_specs=[pl.BlockSpec((tm, tn), lambda i, j: (i, j)),
                   pl.BlockSpec((tm, tn), lambda i, j: (i, j)),
                   pl.BlockSpec((tm, LANES), row),
                   pl.BlockSpec((tm, LANES), row),
                   pl.BlockSpec((slab_o, w_out.shape[1]), step)],
        out_shape=[jax.ShapeDtypeStruct((m, e), BF16),
                   jax.ShapeDtypeStruct((m, e), BF16),
                   jax.ShapeDtypeStruct((m, LANES), F32),
                   jax.ShapeDtypeStruct((m, LANES), F32),
                   jax.ShapeDtypeStruct(w_out.shape, BF16)],
        compiler_params=_params(("parallel", "arbitrary"), 56),
        name="a_in",
    )(h, w_in, w_in, w_in, w_out)


OUT_COL_SPLIT = 4
EPILOGUE_ROWS = 128


def _out_projection_pipeline(step, n_make, make_y, y_buf, wo_ref, x_ref, pg_ref, o_ref,
                             next_gain_ref=None, h_ref=None):
    tm, d = o_ref.shape
    dn = d // OUT_COL_SPLIT
    slot = step % 2

    def project():
        y = y_buf[1 - slot]
        for nb in range(OUT_COL_SPLIT):
            cols = slice(nb * dn, (nb + 1) * dn)
            o_ref[:, cols] += _dot(y, wo_ref[:, cols])

    @pl.when(step == 0)
    def _():
        o_ref[...] = jnp.zeros(o_ref.shape, o_ref.dtype)
        make_y(y_buf.at[slot])

    @pl.when(jnp.logical_and(step > 0, step < n_make))
    def _():
        project()
        make_y(y_buf.at[slot])

    @pl.when(step == n_make)
    def _():
        project()

        def body(r, carry):
            rows = pl.ds(pl.multiple_of(r * EPILOGUE_ROWS, EPILOGUE_ROWS), EPILOGUE_ROWS)
            mm = o_ref[rows, :]
            ms = jnp.mean(mm * mm, axis=-1, keepdims=True)
            xn = x_ref[rows, :] + mm * lax.rsqrt(ms + EPS) * pg_ref[...]
            o_ref[rows, :] = xn
            if h_ref is not None:
                ms_next = jnp.mean(xn * xn, axis=-1, keepdims=True)
                hn = (xn * lax.rsqrt(ms_next + EPS) * next_gain_ref[...]).astype(BF16)
                packed = pl.ds(pl.multiple_of(r * (EPILOGUE_ROWS // 2), EPILOGUE_ROWS // 2),
                               EPILOGUE_ROWS // 2)
                h_ref[packed, :] = pltpu.bitcast(hn, jnp.uint32)
            return carry
        lax.fori_loop(0, tm // EPILOGUE_ROWS, body, 0)


def _a_out_kernel(a_ref, g_ref, s1_ref, s2_ref, lng_ref, lnb_ref, ws_ref, bs_ref,
                  wo_ref, x_ref, pg_ref, ng_ref, o_ref, h_ref, y_buf, *, e):
    k = pl.program_id(1)
    heads = pl.num_programs(1) - 1
    tm = a_ref.shape[0]

    def make_y(y_ref):
        mu = jnp.sum(s1_ref[...], axis=-1, keepdims=True) * (1.0 / e)
        var = jnp.sum(s2_ref[...], axis=-1, keepdims=True) * (1.0 / e) - mu * mu
        rstd = lax.rsqrt(var + EPS)
        g = g_ref[...].astype(F32)
        vn = ((g - mu) * rstd * lng_ref[...] + lnb_ref[...]).astype(BF16)
        w = ws_ref[0]
        t_idx = lax.broadcasted_iota(jnp.int32, w.shape, 0)
        s_idx = lax.broadcasted_iota(jnp.int32, w.shape, 1)
        wc = jnp.where(t_idx >= s_idx, w, 0.0).astype(BF16)
        bs = bs_ref[0]
        for c in range(tm // CHUNK):
            rows = slice(c * CHUNK, (c + 1) * CHUNK)
            sv = _dot(wc, vn[rows, :]) + bs
            y_ref[rows, :] = (a_ref[rows, :].astype(F32) * sv).astype(BF16)

    _out_projection_pipeline(k, heads, make_y, y_buf, wo_ref, x_ref, pg_ref, o_ref,
                             next_gain_ref=ng_ref, h_ref=h_ref)


def _a_out(a, g, s1, s2, ln_g, ln_b, w_s, b_s, w_out, x2d, post_gain, next_gain, tm=512):
    m, e = a.shape
    d = w_out.shape[1]
    heads = w_s.shape[0]
    hd = e // heads
    row = lambda i, k: (i, 0)
    make = lambda k: jnp.minimum(k, heads - 1)
    proj = lambda k: jnp.maximum(k - 1, 0)
    return pl.pallas_call(
        functools.partial(_a_out_kernel, e=e),
        grid=(m // tm, heads + 1),
        in_specs=[pl.BlockSpec((tm, hd), lambda i, k: (i, make(k))),
                  pl.BlockSpec((tm, hd), lambda i, k: (i, make(k))),
                  pl.BlockSpec((tm, LANES), row),
                  pl.BlockSpec((tm, LANES), row),
                  pl.BlockSpec((1, hd), lambda i, k: (0, make(k))),
                  pl.BlockSpec((1, hd), lambda i, k: (0, make(k))),
                  pl.BlockSpec((1, CHUNK, CHUNK), lambda i, k: (make(k), 0, 0)),
                  pl.BlockSpec((1, CHUNK, 1), lambda i, k: (make(k), 0, 0)),
                  pl.BlockSpec((hd, d), lambda i, k: (proj(k), 0)),
                  pl.BlockSpec((tm, d), row),
                  pl.BlockSpec((1, d), lambda i, k: (0, 0)),
                  pl.BlockSpec((1, d), lambda i, k: (0, 0))],
        out_specs=[pl.BlockSpec((tm, d), row),
                   pl.BlockSpec((tm // 2, d), row)],
        out_shape=[jax.ShapeDtypeStruct((m, d), F32),
                   jax.ShapeDtypeStruct((m // 2, d), jnp.uint32)],
        scratch_shapes=[pltpu.VMEM((2, tm, hd), BF16)],
        compiler_params=_params(("parallel", "arbitrary"), 58),
        name="a_out",
    )(a, g, s1, s2, ln_g.reshape(1, e), ln_b.reshape(1, e), w_s,
      b_s.reshape(heads, CHUNK, 1), w_out, x2d, post_gain.reshape(1, d),
      next_gain.reshape(1, d))


def _trailing_window_sums(ext, window):
    s = ext
    span = 1
    while span < window:
        s = s + pltpu.roll(s, span, 0)
        span *= 2
    assert span == window, "pool windows must be powers of two"
    return s


def _b_in_kernel(h_ref, wp_ref, wz_ref, wgf_ref, wof_ref, pooled_ref, sz_ref, wgb_ref,
                 wob_ref, carry_ref, *, seq, group_dim):
    j = pl.program_id(0)
    i = pl.program_id(1)
    tm, tn = pooled_ref.shape
    wgb_ref[...] = wgf_ref[...].astype(BF16)
    wob_ref[...] = wof_ref[...].astype(BF16)

    @pl.when(i == 0)
    def _():
        carry_ref[...] = jnp.zeros(carry_ref.shape, F32)

    h = pltpu.bitcast(h_ref[...], BF16)
    p = _dot(h, wp_ref[...].astype(BF16))
    z = _dot(h, wz_ref[...].astype(BF16))
    sz_ref[...] = _silu(z).astype(sz_ref.dtype)
    tile_in_seq = i % (seq // tm)
    halo = jnp.where(tile_in_seq == 0, 0.0, carry_ref[...])
    carry_ref[...] = p[tm - MAX_WINDOW:, :]
    ext = jnp.concatenate([halo, p], axis=0)
    head = slice(0, MAX_WINDOW)
    pos_head = tile_in_seq * tm + 1 + lax.broadcasted_iota(jnp.int32, (MAX_WINDOW, 1), 0)
    grp = j // (group_dim // tn)
    for gi, window in enumerate(POOL_WINDOWS):
        @pl.when(grp == gi)
        def _(window=window):
            sums = _trailing_window_sums(ext, window)[MAX_WINDOW:, :]
            pooled_ref[...] = (sums * (1.0 / window) - p).astype(pooled_ref.dtype)
            inv_cnt = 1.0 / jnp.minimum(pos_head, window).astype(F32)
            pooled_ref[head, :] = (sums[head, :] * inv_cnt - p[head, :]).astype(pooled_ref.dtype)


def _b_in(h, w_in, w_grp, w_out, e, seq, tm=1024, tn=256):
    m, d = 2 * h.shape[0], h.shape[1]
    nj = e // tn
    ni = m // tm
    group_dim = e // len(POOL_WINDOWS)
    wg2d = w_grp.reshape(-1, w_grp.shape[-1])
    slab_g = wg2d.shape[0] // (nj * ni)
    slab_o = w_out.shape[0] // (nj * ni)
    step = lambda j, i: (j * ni + i, 0)
    return pl.pallas_call(
        functools.partial(_b_in_kernel, seq=seq, group_dim=group_dim),
        grid=(nj, ni),
        in_specs=[pl.BlockSpec((tm // 2, d), lambda j, i: (i, 0)),
                  pl.BlockSpec((d, tn), lambda j, i: (0, j)),
                  pl.BlockSpec((d, tn), lambda j, i: (0, nj + j)),
                  pl.BlockSpec((slab_g, wg2d.shape[1]), step),
                  pl.BlockSpec((slab_o, w_out.shape[1]), step)],
        out_specs=[pl.BlockSpec((tm, tn), lambda j, i: (i, j)),
                   pl.BlockSpec((tm, tn), lambda j, i: (i, j)),
                   pl.BlockSpec((slab_g, wg2d.shape[1]), step),
                   pl.BlockSpec((slab_o, w_out.shape[1]), step)],
        out_shape=[jax.ShapeDtypeStruct((m, e), BF16),
                   jax.ShapeDtypeStruct((m, e), BF16),
                   jax.ShapeDtypeStruct(wg2d.shape, BF16),
                   jax.ShapeDtypeStruct(w_out.shape, BF16)],
        scratch_shapes=[pltpu.VMEM((MAX_WINDOW, tn), F32)],
        compiler_params=_params(("arbitrary", "arbitrary"), 56),
        name="b_in",
    )(h, w_in, w_in, wg2d, w_out)


def _b_out_kernel(pooled_ref, wg_ref, bg_ref, sc_ref, sz_ref, wo_ref, x_ref, pg_ref,
                  o_ref, y_buf):
    s = pl.program_id(1)
    n_blocks = pl.num_programs(1) - 1

    def make_y(y_ref):
        mixed = _dot(pooled_ref[...], wg_ref[0]) + bg_ref[...]
        y_ref[...] = (mixed * sc_ref[...] * sz_ref[...].astype(F32)).astype(BF16)

    _out_projection_pipeline(s, n_blocks, make_y, y_buf, wo_ref, x_ref, pg_ref, o_ref)


def _b_out(pooled, w_grp, b_grp, scale, sz, w_out, x2d, post_gain, tm=512, tn=512):
    m, e = pooled.shape
    d = w_out.shape[1]
    groups, gw, _ = w_grp.shape
    nn = gw // tn
    n_blocks = groups * nn
    row = lambda i, s: (i, 0)
    make = lambda s: jnp.minimum(s, n_blocks - 1)
    proj = lambda s: jnp.maximum(s - 1, 0)
    return pl.pallas_call(
        _b_out_kernel,
        grid=(m // tm, n_blocks + 1),
        in_specs=[pl.BlockSpec((tm, gw), lambda i, s: (i, make(s) // nn)),
                  pl.BlockSpec((1, gw, tn), lambda i, s: (make(s) // nn, 0, make(s) % nn)),
                  pl.BlockSpec((1, tn), lambda i, s: (0, make(s))),
                  pl.BlockSpec((1, tn), lambda i, s: (0, make(s))),
                  pl.BlockSpec((tm, tn), lambda i, s: (i, make(s))),
                  pl.BlockSpec((tn, d), lambda i, s: (proj(s), 0)),
                  pl.BlockSpec((tm, d), row),
                  pl.BlockSpec((1, d), lambda i, s: (0, 0))],
        out_specs=pl.BlockSpec((tm, d), row),
        out_shape=jax.ShapeDtypeStruct((m, d), F32),
        scratch_shapes=[pltpu.VMEM((2, tm, tn), BF16)],
        compiler_params=_params(("parallel", "arbitrary"), 56),
        name="b_out",
    )(pooled, w_grp, b_grp.reshape(1, e), scale.reshape(1, e), sz, w_out, x2d,
      post_gain.reshape(1, d))


def kernel(x, pre_norm, post_norm, a_w_in, a_ln_g, a_ln_b, a_w_s, a_b_s, a_w_out,
           b_w_in, b_w_grp, b_b_grp, b_scale, b_w_out):
    bsz, seq, d = x.shape
    e = a_w_out.shape[1]
    x2d = x.reshape(bsz * seq, d)

    h = _prenorm(x2d, pre_norm[0])
    a, g, s1, s2, a_wo = _a_in(h, a_w_in[0], a_w_out[0], e)
    x2d, h = _a_out(a, g, s1, s2, a_ln_g[0], a_ln_b[0], a_w_s[0], a_b_s[0],
                    a_wo, x2d, post_norm[0], pre_norm[1])

    pooled, sz, b_wg, b_wo = _b_in(h, b_w_in[0], b_w_grp[0], b_w_out[0], e, seq)
    x2d = _b_out(pooled, b_wg.reshape(b_w_grp[0].shape), b_b_grp[0], b_scale[0], sz,
                 b_wo, x2d, post_norm[1])
    return x2d.reshape(bsz, seq, d)
```

```python
import functools
import math

import jax
import jax.numpy as jnp
from jax import lax
from jax.experimental import pallas as pl
from jax.experimental.pallas import tpu as pltpu

EPS = 1e-6
CHUNK = 128
POOL_WINDOWS = (2, 4, 8, 16)
MAX_WINDOW = max(POOL_WINDOWS)
LANES = 128
MIB = 1024 * 1024

F32 = jnp.float32
BF16 = jnp.bfloat16


def _params(semantics, vmem_mib):
    return pltpu.CompilerParams(dimension_semantics=semantics,
                                vmem_limit_bytes=vmem_mib * MIB)


def _gelu(x):
    return 0.5 * x * (1.0 + lax.erf(x * math.sqrt(0.5)))


def _silu(x):
    return x / (1.0 + jnp.exp(-x))


def _dot(a, b):
    return jnp.dot(a, b, preferred_element_type=F32)


def _prenorm_kernel(x_ref, g_ref, h_ref):
    x = x_ref[...]
    ms = jnp.mean(x * x, axis=-1, keepdims=True)
    h = (x * lax.rsqrt(ms + EPS) * g_ref[...]).astype(BF16)
    h_ref[...] = pltpu.bitcast(h, jnp.uint32)


def _prenorm(x2d, gain, tm=512):
    m, d = x2d.shape
    return pl.pallas_call(
        _prenorm_kernel,
        grid=(m // tm,),
        in_specs=[pl.BlockSpec((tm, d), lambda i: (i, 0)),
                  pl.BlockSpec((1, d), lambda i: (0, 0))],
        out_specs=pl.BlockSpec((tm // 2, d), lambda i: (i, 0)),
        out_shape=jax.ShapeDtypeStruct((m // 2, d), jnp.uint32),
        compiler_params=_params(("parallel",), 40),
        name="prenorm",
    )(x2d, gain.reshape(1, d))


def _a_in_kernel(h_ref, wu_ref, wv_ref, wz_ref, wof_ref, wnf_ref, a_ref, g_ref, s1_ref,
                 s2_ref, wob_ref, wnb_ref):
    j = pl.program_id(1)
    wob_ref[...] = wof_ref[...].astype(BF16)
    wnb_ref[...] = wnf_ref[...].astype(BF16)
    h = pltpu.bitcast(h_ref[...], BF16)
    u = _dot(h, wu_ref[...].astype(BF16))
    v = _dot(h, wv_ref[...].astype(BF16))
    z = _dot(h, wz_ref[...].astype(BF16))
    a_ref[...] = (_gelu(u) * _silu(z)).astype(a_ref.dtype)
    gv = _gelu(v)
    g_ref[...] = gv.astype(g_ref.dtype)
    tn = gv.shape[1]
    p1 = gv[:, 0:LANES]
    p2 = p1 * p1
    for k in range(1, tn // LANES):
        blk = gv[:, k * LANES:(k + 1) * LANES]
        p1 = p1 + blk
        p2 = p2 + blk * blk

    @pl.when(j == 0)
    def _():
        s1_ref[...] = p1
        s2_ref[...] = p2

    @pl.when(j > 0)
    def _():
        s1_ref[...] += p1
        s2_ref[...] += p2


def _a_in(h, w_in, w_out, w_next, e, tm=1024, tn=256):
    m, d = 2 * h.shape[0], h.shape[1]
    nj = e // tn
    ni = m // tm
    slab_o = w_out.shape[0] // (ni * nj)
    slab_n = w_next.shape[0] // (ni * nj)
    row = lambda i, j: (i, 0)
    step = lambda i, j: (i * nj + j, 0)
    return pl.pallas_call(
        _a_in_kernel,
        grid=(ni, nj),
        in_specs=[pl.BlockSpec((tm // 2, d), row),
                  pl.BlockSpec((d, tn), lambda i, j: (0, j)),
                  pl.BlockSpec((d, tn), lambda i, j: (0, nj + j)),
                  pl.BlockSpec((d, tn), lambda i, j: (0, 2 * nj + j)),
                  pl.BlockSpec((slab_o, w_out.shape[1]), step),
                  pl.BlockSpec((slab_n, w_next.shape[1]), step)],
        out_specs=[pl.BlockSpec((tm, tn), lambda i, j: (i, j)),
                   pl.BlockSpec((tm, tn), lambda i, j: (i, j)),
                   pl.BlockSpec((tm, LANES), row),
                   pl.BlockSpec((tm, LANES), row),
                   pl.BlockSpec((slab_o, w_out.shape[1]), step),
                   pl.BlockSpec((slab_n, w_next.shape[1]), step)],
        out_shape=[jax.ShapeDtypeStruct((m, e), BF16),
                   jax.ShapeDtypeStruct((m, e), BF16),
                   jax.ShapeDtypeStruct((m, LANES), F32),
                   jax.ShapeDtypeStruct((m, LANES), F32),
                   jax.ShapeDtypeStruct(w_out.shape, BF16),
                   jax.ShapeDtypeStruct(w_next.shape, BF16)],
        compiler_params=_params(("parallel", "arbitrary"), 56),
        name="a_in",
    )(h, w_in, w_in, w_in, w_out, w_next)


OUT_COL_SPLIT = 4
EPILOGUE_ROWS = 128


class _PipelineIndex:
    def __init__(self, n_rows, n_blocks):
        self.n_blocks = n_blocks
        self.total = n_rows * n_blocks

    def made(self, t):
        tt = jnp.minimum(t, self.total - 1)
        return tt // self.n_blocks, tt % self.n_blocks

    def projected(self, t):
        tt = jnp.maximum(t - 1, 0)
        return tt // self.n_blocks, tt % self.n_blocks


def _out_projection_pipeline(idx, make_y, y_buf, wo_ref, x_ref, pg_ref, o_ref,
                             next_gain_ref=None, h_ref=None):
    t = pl.program_id(0)
    tm, d = o_ref.shape
    dn = d // OUT_COL_SPLIT
    slot = t % 2
    _, blk_p = idx.projected(t)

    def project():
        y = y_buf[1 - slot]
        for nb in range(OUT_COL_SPLIT):
            cols = slice(nb * dn, (nb + 1) * dn)
            o_ref[:, cols] += _dot(y, wo_ref[:, cols])

    @pl.when(jnp.logical_and(t > 0, blk_p == 0))
    def _():
        o_ref[...] = jnp.zeros(o_ref.shape, o_ref.dtype)

    @pl.when(t == 0)
    def _():
        make_y(y_buf.at[slot])

    @pl.when(jnp.logical_and(t > 0, t < idx.total))
    def _():
        project()
        make_y(y_buf.at[slot])

    @pl.when(t == idx.total)
    def _():
        project()

    @pl.when(jnp.logical_and(t > 0, blk_p == idx.n_blocks - 1))
    def _():
        def body(r, carry):
            rows = pl.ds(pl.multiple_of(r * EPILOGUE_ROWS, EPILOGUE_ROWS), EPILOGUE_ROWS)
            mm = o_ref[rows, :]
            ms = jnp.mean(mm * mm, axis=-1, keepdims=True)
            xn = x_ref[rows, :] + mm * lax.rsqrt(ms + EPS) * pg_ref[...]
            o_ref[rows, :] = xn
            if h_ref is not None:
                ms_next = jnp.mean(xn * xn, axis=-1, keepdims=True)
                hn = (xn * lax.rsqrt(ms_next + EPS) * next_gain_ref[...]).astype(BF16)
                packed = pl.ds(pl.multiple_of(r * (EPILOGUE_ROWS // 2), EPILOGUE_ROWS // 2),
                               EPILOGUE_ROWS // 2)
                h_ref[packed, :] = pltpu.bitcast(hn, jnp.uint32)
            return carry
        lax.fori_loop(0, tm // EPILOGUE_ROWS, body, 0)


def _a_out_kernel(a_ref, g_ref, s1_ref, s2_ref, lng_ref, lnb_ref, ws_ref, bs_ref,
                  wo_ref, x_ref, pg_ref, ng_ref, o_ref, h_ref, y_buf, *, e, idx):
    tm = a_ref.shape[0]

    def make_y(y_ref):
        mu = jnp.sum(s1_ref[...], axis=-1, keepdims=True) * (1.0 / e)
        var = jnp.sum(s2_ref[...], axis=-1, keepdims=True) * (1.0 / e) - mu * mu
        rstd = lax.rsqrt(var + EPS)
        g = g_ref[...].astype(F32)
        vn = ((g - mu) * rstd * lng_ref[...] + lnb_ref[...]).astype(BF16)
        w = ws_ref[0]
        t_idx = lax.broadcasted_iota(jnp.int32, w.shape, 0)
        s_idx = lax.broadcasted_iota(jnp.int32, w.shape, 1)
        wc = jnp.where(t_idx >= s_idx, w, 0.0).astype(BF16)
        bs = bs_ref[0]
        for c in range(tm // CHUNK):
            rows = slice(c * CHUNK, (c + 1) * CHUNK)
            sv = _dot(wc, vn[rows, :]) + bs
            y_ref[rows, :] = (a_ref[rows, :].astype(F32) * sv).astype(BF16)

    _out_projection_pipeline(idx, make_y, y_buf, wo_ref, x_ref, pg_ref, o_ref,
                             next_gain_ref=ng_ref, h_ref=h_ref)


def _a_out(a, g, s1, s2, ln_g, ln_b, w_s, b_s, w_out, x2d, post_gain, next_gain, tm=512):
    m, e = a.shape
    d = w_out.shape[1]
    heads = w_s.shape[0]
    hd = e // heads
    idx = _PipelineIndex(m // tm, heads)
    made_rc = lambda t: idx.made(t)
    made_row = lambda t: (idx.made(t)[0], 0)
    made_col = lambda t: (0, idx.made(t)[1])
    made_head = lambda t: (idx.made(t)[1], 0, 0)
    proj_row = lambda t: (idx.projected(t)[0], 0)
    const = lambda t: (0, 0)
    return pl.pallas_call(
        functools.partial(_a_out_kernel, e=e, idx=idx),
        grid=(idx.total + 1,),
        in_specs=[pl.BlockSpec((tm, hd), made_rc),
                  pl.BlockSpec((tm, hd), made_rc),
                  pl.BlockSpec((tm, LANES), made_row),
                  pl.BlockSpec((tm, LANES), made_row),
                  pl.BlockSpec((1, hd), made_col),
                  pl.BlockSpec((1, hd), made_col),
                  pl.BlockSpec((1, CHUNK, CHUNK), made_head),
                  pl.BlockSpec((1, CHUNK, 1), made_head),
                  pl.BlockSpec((hd, d), lambda t: (idx.projected(t)[1], 0)),
                  pl.BlockSpec((tm, d), proj_row),
                  pl.BlockSpec((1, d), const),
                  pl.BlockSpec((1, d), const)],
        out_specs=[pl.BlockSpec((tm, d), proj_row),
                   pl.BlockSpec((tm // 2, d), proj_row)],
        out_shape=[jax.ShapeDtypeStruct((m, d), F32),
                   jax.ShapeDtypeStruct((m // 2, d), jnp.uint32)],
        scratch_shapes=[pltpu.VMEM((2, tm, hd), BF16)],
        compiler_params=_params(("arbitrary",), 58),
        name="a_out",
    )(a, g, s1, s2, ln_g.reshape(1, e), ln_b.reshape(1, e), w_s,
      b_s.reshape(heads, CHUNK, 1), w_out, x2d, post_gain.reshape(1, d),
      next_gain.reshape(1, d))


def _trailing_window_sums(ext, window):
    s = ext
    span = 1
    while span < window:
        s = s + pltpu.roll(s, span, 0)
        span *= 2
    assert span == window, "pool windows must be powers of two"
    return s


def _b_in_kernel(h_ref, wp_ref, wz_ref, wgf_ref, wof_ref, pooled_ref, sz_ref, wgb_ref,
                 wob_ref, carry_ref, *, seq, group_dim):
    j = pl.program_id(0)
    i = pl.program_id(1)
    tm, tn = pooled_ref.shape
    wgb_ref[...] = wgf_ref[...].astype(BF16)
    wob_ref[...] = wof_ref[...].astype(BF16)

    @pl.when(i == 0)
    def _():
        carry_ref[...] = jnp.zeros(carry_ref.shape, F32)

    h = pltpu.bitcast(h_ref[...], BF16)
    p = _dot(h, wp_ref[...])
    z = _dot(h, wz_ref[...])
    sz_ref[...] = _silu(z).astype(sz_ref.dtype)
    tile_in_seq = i % (seq // tm)
    halo = jnp.where(tile_in_seq == 0, 0.0, carry_ref[...])
    carry_ref[...] = p[tm - MAX_WINDOW:, :]
    ext = jnp.concatenate([halo, p], axis=0)
    head = slice(0, MAX_WINDOW)
    pos_head = tile_in_seq * tm + 1 + lax.broadcasted_iota(jnp.int32, (MAX_WINDOW, 1), 0)
    grp = j // (group_dim // tn)
    for gi, window in enumerate(POOL_WINDOWS):
        @pl.when(grp == gi)
        def _(window=window):
            sums = _trailing_window_sums(ext, window)[MAX_WINDOW:, :]
            pooled_ref[...] = (sums * (1.0 / window) - p).astype(pooled_ref.dtype)
            inv_cnt = 1.0 / jnp.minimum(pos_head, window).astype(F32)
            pooled_ref[head, :] = (sums[head, :] * inv_cnt - p[head, :]).astype(pooled_ref.dtype)


def _b_in(h, w_in, w_grp, w_out, e, seq, tm=1024, tn=512):
    m, d = 2 * h.shape[0], h.shape[1]
    nj = e // tn
    ni = m // tm
    group_dim = e // len(POOL_WINDOWS)
    wg2d = w_grp.reshape(-1, w_grp.shape[-1])
    slab_g = wg2d.shape[0] // (nj * ni)
    slab_o = w_out.shape[0] // (nj * ni)
    step = lambda j, i: (j * ni + i, 0)
    return pl.pallas_call(
        functools.partial(_b_in_kernel, seq=seq, group_dim=group_dim),
        grid=(nj, ni),
        in_specs=[pl.BlockSpec((tm // 2, d), lambda j, i: (i, 0)),
                  pl.BlockSpec((d, tn), lambda j, i: (0, j)),
                  pl.BlockSpec((d, tn), lambda j, i: (0, nj + j)),
                  pl.BlockSpec((slab_g, wg2d.shape[1]), step),
                  pl.BlockSpec((slab_o, w_out.shape[1]), step)],
        out_specs=[pl.BlockSpec((tm, tn), lambda j, i: (i, j)),
                   pl.BlockSpec((tm, tn), lambda j, i: (i, j)),
                   pl.BlockSpec((slab_g, wg2d.shape[1]), step),
                   pl.BlockSpec((slab_o, w_out.shape[1]), step)],
        out_shape=[jax.ShapeDtypeStruct((m, e), BF16),
                   jax.ShapeDtypeStruct((m, e), BF16),
                   jax.ShapeDtypeStruct(wg2d.shape, BF16),
                   jax.ShapeDtypeStruct(w_out.shape, BF16)],
        scratch_shapes=[pltpu.VMEM((MAX_WINDOW, tn), F32)],
        compiler_params=_params(("arbitrary", "arbitrary"), 58),
        name="b_in",
    )(h, w_in, w_in, wg2d, w_out)


def _b_out_kernel(pooled_ref, wg_ref, bg_ref, sc_ref, sz_ref, wo_ref, x_ref, pg_ref,
                  o_ref, y_buf, *, idx):
    def make_y(y_ref):
        mixed = _dot(pooled_ref[...], wg_ref[0]) + bg_ref[...]
        y_ref[...] = (mixed * sc_ref[...] * sz_ref[...].astype(F32)).astype(BF16)

    _out_projection_pipeline(idx, make_y, y_buf, wo_ref, x_ref, pg_ref, o_ref)


def _b_out(pooled, w_grp, b_grp, scale, sz, w_out, x2d, post_gain, tm=512, tn=512):
    m, e = pooled.shape
    d = w_out.shape[1]
    groups, gw, _ = w_grp.shape
    nn = gw // tn
    idx = _PipelineIndex(m // tm, groups * nn)
    made_rc = lambda t: idx.made(t)
    made_col = lambda t: (0, idx.made(t)[1])
    proj_row = lambda t: (idx.projected(t)[0], 0)
    return pl.pallas_call(
        functools.partial(_b_out_kernel, idx=idx),
        grid=(idx.total + 1,),
        in_specs=[pl.BlockSpec((tm, gw), lambda t: (idx.made(t)[0], idx.made(t)[1] // nn)),
                  pl.BlockSpec((1, gw, tn),
                               lambda t: (idx.made(t)[1] // nn, 0, idx.made(t)[1] % nn)),
                  pl.BlockSpec((1, tn), made_col),
                  pl.BlockSpec((1, tn), made_col),
                  pl.BlockSpec((tm, tn), made_rc),
                  pl.BlockSpec((tn, d), lambda t: (idx.projected(t)[1], 0)),
                  pl.BlockSpec((tm, d), proj_row),
                  pl.BlockSpec((1, d), lambda t: (0, 0))],
        out_specs=pl.BlockSpec((tm, d), proj_row),
        out_shape=jax.ShapeDtypeStruct((m, d), F32),
        scratch_shapes=[pltpu.VMEM((2, tm, tn), BF16)],
        compiler_params=_params(("arbitrary",), 56),
        name="b_out",
    )(pooled, w_grp, b_grp.reshape(1, e), scale.reshape(1, e), sz, w_out, x2d,
      post_gain.reshape(1, d))


def kernel(x, pre_norm, post_norm, a_w_in, a_ln_g, a_ln_b, a_w_s, a_b_s, a_w_out,
           b_w_in, b_w_grp, b_b_grp, b_scale, b_w_out):
    bsz, seq, d = x.shape
    e = a_w_out.shape[1]
    x2d = x.reshape(bsz * seq, d)

    h = _prenorm(x2d, pre_norm[0])
    a, g, s1, s2, a_wo, b_wi = _a_in(h, a_w_in[0], a_w_out[0], b_w_in[0], e)
    x2d, h = _a_out(a, g, s1, s2, a_ln_g[0], a_ln_b[0], a_w_s[0], a_b_s[0],
                    a_wo, x2d, post_norm[0], pre_norm[1])

    pooled, sz, b_wg, b_wo = _b_in(h, b_wi, b_w_grp[0], b_w_out[0], e, seq)
    x2d = _b_out(pooled, b_wg.reshape(b_w_grp[0].shape), b_b_grp[0], b_scale[0], sz,
                 b_wo, x2d, post_norm[1])
    return x2d.reshape(bsz, seq, d)
```

```python
import functools
import math

import jax
import jax.numpy as jnp
from jax import lax
from jax.experimental import pallas as pl
from jax.experimental.pallas import tpu as pltpu

EPS = 1e-6
CHUNK = 128
POOL_WINDOWS = (2, 4, 8, 16)
MAX_WINDOW = max(POOL_WINDOWS)
LANES = 128
MIB = 1024 * 1024

F32 = jnp.float32
BF16 = jnp.bfloat16


def _params(semantics, vmem_mib):
    return pltpu.CompilerParams(dimension_semantics=semantics,
                                vmem_limit_bytes=vmem_mib * MIB)


def _gelu(x):
    return 0.5 * x * (1.0 + lax.erf(x * math.sqrt(0.5)))


def _silu(x):
    return x / (1.0 + jnp.exp(-x))


def _dot(a, b):
    return jnp.dot(a, b, preferred_element_type=F32)


def _prenorm_kernel(x_ref, g_ref, h_ref):
    x = x_ref[...]
    ms = jnp.mean(x * x, axis=-1, keepdims=True)
    h = (x * lax.rsqrt(ms + EPS) * g_ref[...]).astype(BF16)
    h_ref[...] = pltpu.bitcast(h, jnp.uint32)


def _prenorm(x2d, gain, tm=512):
    m, d = x2d.shape
    return pl.pallas_call(
        _prenorm_kernel,
        grid=(m // tm,),
        in_specs=[pl.BlockSpec((tm, d), lambda i: (i, 0)),
                  pl.BlockSpec((1, d), lambda i: (0, 0))],
        out_specs=pl.BlockSpec((tm // 2, d), lambda i: (i, 0)),
        out_shape=jax.ShapeDtypeStruct((m // 2, d), jnp.uint32),
        compiler_params=_params(("parallel",), 40),
        name="prenorm",
    )(x2d, gain.reshape(1, d))


def _a_in_kernel(h_ref, wu_ref, wv_ref, wz_ref, wof_ref, wnf_ref, a_ref, g_ref, s1_ref,
                 s2_ref, wob_ref, wnb_ref):
    j = pl.program_id(1)
    wob_ref[...] = wof_ref[...].astype(BF16)
    wnb_ref[...] = wnf_ref[...].astype(BF16)
    h = pltpu.bitcast(h_ref[...], BF16)
    u = _dot(h, wu_ref[...].astype(BF16))
    v = _dot(h, wv_ref[...].astype(BF16))
    z = _dot(h, wz_ref[...].astype(BF16))
    a_ref[...] = (_gelu(u) * _silu(z)).astype(a_ref.dtype)
    gv = _gelu(v)
    g_ref[...] = gv.astype(g_ref.dtype)
    tn = gv.shape[1]
    p1 = gv[:, 0:LANES]
    p2 = p1 * p1
    for k in range(1, tn // LANES):
        blk = gv[:, k * LANES:(k + 1) * LANES]
        p1 = p1 + blk
        p2 = p2 + blk * blk

    @pl.when(j == 0)
    def _():
        s1_ref[...] = p1
        s2_ref[...] = p2

    @pl.when(j > 0)
    def _():
        s1_ref[...] += p1
        s2_ref[...] += p2


def _a_in(h, w_in, w_out, w_next, e, tm=1024, tn=256):
    m, d = 2 * h.shape[0], h.shape[1]
    nj = e // tn
    ni = m // tm
    slab_o = w_out.shape[0] // (ni * nj)
    slab_n = w_next.shape[0] // (ni * nj)
    row = lambda i, j: (i, 0)
    step = lambda i, j: (i * nj + j, 0)
    return pl.pallas_call(
        _a_in_kernel,
        grid=(ni, nj),
        in_specs=[pl.BlockSpec((tm // 2, d), row),
                  pl.BlockSpec((d, tn), lambda i, j: (0, j)),
                  pl.BlockSpec((d, tn), lambda i, j: (0, nj + j)),
                  pl.BlockSpec((d, tn), lambda i, j: (0, 2 * nj + j)),
                  pl.BlockSpec((slab_o, w_out.shape[1]), step),
                  pl.BlockSpec((slab_n, w_next.shape[1]), step)],
        out_specs=[pl.BlockSpec((tm, tn), lambda i, j: (i, j)),
                   pl.BlockSpec((tm, tn), lambda i, j: (i, j)),
                   pl.BlockSpec((tm, LANES), row),
                   pl.BlockSpec((tm, LANES), row),
                   pl.BlockSpec((slab_o, w_out.shape[1]), step),
                   pl.BlockSpec((slab_n, w_next.shape[1]), step)],
        out_shape=[jax.ShapeDtypeStruct((m, e), BF16),
                   jax.ShapeDtypeStruct((m, e), BF16),
                   jax.ShapeDtypeStruct((m, LANES), F32),
                   jax.ShapeDtypeStruct((m, LANES), F32),
                   jax.ShapeDtypeStruct(w_out.shape, BF16),
                   jax.ShapeDtypeStruct(w_next.shape, BF16)],
        compiler_params=_params(("parallel", "arbitrary"), 56),
        name="a_in",
    )(h, w_in, w_in, w_in, w_out, w_next)


OUT_COL_SPLIT = 4
EPILOGUE_ROWS = 128


class _PipelineIndex:
    def __init__(self, n_rows, n_blocks):
        self.n_blocks = n_blocks
        self.total = n_rows * n_blocks

    def made(self, t):
        tt = jnp.minimum(t, self.total - 1)
        return tt // self.n_blocks, tt % self.n_blocks

    def projected(self, t):
        tt = jnp.maximum(t - 1, 0)
        return tt // self.n_blocks, tt % self.n_blocks


def _out_projection_pipeline(idx, make_y, y_buf, wo_ref, x_ref, pg_ref, o_ref,
                             next_gain_ref=None, h_ref=None):
    t = pl.program_id(0)
    tm, d = o_ref.shape
    dn = d // OUT_COL_SPLIT
    slot = t % 2
    _, blk_p = idx.projected(t)

    def project(first_block):
        y = y_buf[1 - slot]
        for nb in range(OUT_COL_SPLIT):
            cols = slice(nb * dn, (nb + 1) * dn)
            if first_block:
                o_ref[:, cols] = _dot(y, wo_ref[:, cols])
            else:
                o_ref[:, cols] += _dot(y, wo_ref[:, cols])

    @pl.when(t == 0)
    def _():
        make_y(y_buf.at[slot])

    for first_block in (True, False):
        @pl.when(jnp.logical_and(jnp.logical_and(t > 0, t < idx.total),
                                 (blk_p == 0) == first_block))
        def _(first_block=first_block):
            project(first_block)
            make_y(y_buf.at[slot])

    assert idx.n_blocks > 1
    @pl.when(t == idx.total)
    def _():
        project(False)

    @pl.when(jnp.logical_and(t > 0, blk_p == idx.n_blocks - 1))
    def _():
        def body(r, carry):
            rows = pl.ds(pl.multiple_of(r * EPILOGUE_ROWS, EPILOGUE_ROWS), EPILOGUE_ROWS)
            mm = o_ref[rows, :]
            ms = jnp.mean(mm * mm, axis=-1, keepdims=True)
            xn = x_ref[rows, :] + mm * lax.rsqrt(ms + EPS) * pg_ref[...]
            o_ref[rows, :] = xn
            if h_ref is not None:
                ms_next = jnp.mean(xn * xn, axis=-1, keepdims=True)
                hn = (xn * lax.rsqrt(ms_next + EPS) * next_gain_ref[...]).astype(BF16)
                packed = pl.ds(pl.multiple_of(r * (EPILOGUE_ROWS // 2), EPILOGUE_ROWS // 2),
                               EPILOGUE_ROWS // 2)
                h_ref[packed, :] = pltpu.bitcast(hn, jnp.uint32)
            return carry
        lax.fori_loop(0, tm // EPILOGUE_ROWS, body, 0)


def _a_out_kernel(a_ref, g_ref, s1_ref, s2_ref, lng_ref, lnb_ref, ws_ref, bs_ref,
                  wo_ref, x_ref, pg_ref, ng_ref, o_ref, h_ref, y_buf, *, e, idx):
    tm = a_ref.shape[0]

    def make_y(y_ref):
        mu = jnp.sum(s1_ref[...], axis=-1, keepdims=True) * (1.0 / e)
        var = jnp.sum(s2_ref[...], axis=-1, keepdims=True) * (1.0 / e) - mu * mu
        rstd = lax.rsqrt(var + EPS)
        g = g_ref[...].astype(F32)
        vn = ((g - mu) * rstd * lng_ref[...] + lnb_ref[...]).astype(BF16)
        w = ws_ref[0]
        t_idx = lax.broadcasted_iota(jnp.int32, w.shape, 0)
        s_idx = lax.broadcasted_iota(jnp.int32, w.shape, 1)
        wc = jnp.where(t_idx >= s_idx, w, 0.0).astype(BF16)
        n_chunks = tm // CHUNK
        zero = jnp.zeros_like(wc)
        w_bd = jnp.concatenate(
            [jnp.concatenate([wc if c == r else zero for c in range(n_chunks)], axis=1)
             for r in range(n_chunks)], axis=0)
        bs = jnp.concatenate([bs_ref[0]] * n_chunks, axis=0)
        sv = _dot(w_bd, vn) + bs
        y_ref[...] = (a_ref[...].astype(F32) * sv).astype(BF16)

    _out_projection_pipeline(idx, make_y, y_buf, wo_ref, x_ref, pg_ref, o_ref,
                             next_gain_ref=ng_ref, h_ref=h_ref)


def _a_out(a, g, s1, s2, ln_g, ln_b, w_s, b_s, w_out, x2d, post_gain, next_gain, tm=512):
    m, e = a.shape
    d = w_out.shape[1]
    heads = w_s.shape[0]
    hd = e // heads
    idx = _PipelineIndex(m // tm, heads)
    made_rc = lambda t: idx.made(t)
    made_row = lambda t: (idx.made(t)[0], 0)
    made_col = lambda t: (0, idx.made(t)[1])
    made_head = lambda t: (idx.made(t)[1], 0, 0)
    proj_row = lambda t: (idx.projected(t)[0], 0)
    const = lambda t: (0, 0)
    return pl.pallas_call(
        functools.partial(_a_out_kernel, e=e, idx=idx),
        grid=(idx.total + 1,),
        in_specs=[pl.BlockSpec((tm, hd), made_rc),
                  pl.BlockSpec((tm, hd), made_rc),
                  pl.BlockSpec((tm, LANES), made_row),
                  pl.BlockSpec((tm, LANES), made_row),
                  pl.BlockSpec((1, hd), made_col),
                  pl.BlockSpec((1, hd), made_col),
                  pl.BlockSpec((1, CHUNK, CHUNK), made_head),
                  pl.BlockSpec((1, CHUNK, 1), made_head),
                  pl.BlockSpec((hd, d), lambda t: (idx.projected(t)[1], 0)),
                  pl.BlockSpec((tm, d), proj_row),
                  pl.BlockSpec((1, d), const),
                  pl.BlockSpec((1, d), const)],
        out_specs=[pl.BlockSpec((tm, d), proj_row),
                   pl.BlockSpec((tm // 2, d), proj_row)],
        out_shape=[jax.ShapeDtypeStruct((m, d), F32),
                   jax.ShapeDtypeStruct((m // 2, d), jnp.uint32)],
        scratch_shapes=[pltpu.VMEM((2, tm, hd), BF16)],
        compiler_params=_params(("arbitrary",), 58),
        name="a_out",
    )(a, g, s1, s2, ln_g.reshape(1, e), ln_b.reshape(1, e), w_s,
      b_s.reshape(heads, CHUNK, 1), w_out, x2d, post_gain.reshape(1, d),
      next_gain.reshape(1, d))


def _trailing_window_sums(ext, window):
    s = ext
    span = 1
    while span < window:
        s = s + pltpu.roll(s, span, 0)
        span *= 2
    assert span == window, "pool windows must be powers of two"
    return s


def _b_in_kernel(h_ref, wp_ref, wz_ref, wgf_ref, wof_ref, pooled_ref, sz_ref, wgb_ref,
                 wob_ref, carry_ref, *, seq, group_dim):
    j = pl.program_id(0)
    i = pl.program_id(1)
    tm, tn = pooled_ref.shape
    wgb_ref[...] = wgf_ref[...].astype(BF16)
    wob_ref[...] = wof_ref[...].astype(BF16)

    @pl.when(i == 0)
    def _():
        carry_ref[...] = jnp.zeros(carry_ref.shape, F32)

    h = pltpu.bitcast(h_ref[...], BF16)
    p = _dot(h, wp_ref[...])
    z = _dot(h, wz_ref[...])
    sz_ref[...] = _silu(z).astype(sz_ref.dtype)
    tile_in_seq = i % (seq // tm)
    halo = jnp.where(tile_in_seq == 0, 0.0, carry_ref[...])
    carry_ref[...] = p[tm - MAX_WINDOW:, :]
    ext = jnp.concatenate([halo, p], axis=0)
    head = slice(0, MAX_WINDOW)
    pos_head = tile_in_seq * tm + 1 + lax.broadcasted_iota(jnp.int32, (MAX_WINDOW, 1), 0)
    grp = j // (group_dim // tn)
    for gi, window in enumerate(POOL_WINDOWS):
        @pl.when(grp == gi)
        def _(window=window):
            sums = _trailing_window_sums(ext, window)[MAX_WINDOW:, :]
            pooled_ref[...] = (sums * (1.0 / window) - p).astype(pooled_ref.dtype)
            inv_cnt = 1.0 / jnp.minimum(pos_head, window).astype(F32)
            pooled_ref[head, :] = (sums[head, :] * inv_cnt - p[head, :]).astype(pooled_ref.dtype)


def _b_in(h, w_in, w_grp, w_out, e, seq, tm=1024, tn=512):
    m, d = 2 * h.shape[0], h.shape[1]
    nj = e // tn
    ni = m // tm
    group_dim = e // len(POOL_WINDOWS)
    wg2d = w_grp.reshape(-1, w_grp.shape[-1])
    slab_g = wg2d.shape[0] // (nj * ni)
    slab_o = w_out.shape[0] // (nj * ni)
    step = lambda j, i: (j * ni + i, 0)
    return pl.pallas_call(
        functools.partial(_b_in_kernel, seq=seq, group_dim=group_dim),
        grid=(nj, ni),
        in_specs=[pl.BlockSpec((tm // 2, d), lambda j, i: (i, 0)),
                  pl.BlockSpec((d, tn), lambda j, i: (0, j)),
                  pl.BlockSpec((d, tn), lambda j, i: (0, nj + j)),
                  pl.BlockSpec((slab_g, wg2d.shape[1]), step),
                  pl.BlockSpec((slab_o, w_out.shape[1]), step)],
        out_specs=[pl.BlockSpec((tm, tn), lambda j, i: (i, j)),
                   pl.BlockSpec((tm, tn), lambda j, i: (i, j)),
                   pl.BlockSpec((slab_g, wg2d.shape[1]), step),
                   pl.BlockSpec((slab_o, w_out.shape[1]), step)],
        out_shape=[jax.ShapeDtypeStruct((m, e), BF16),
                   jax.ShapeDtypeStruct((m, e), BF16),
                   jax.ShapeDtypeStruct(wg2d.shape, BF16),
                   jax.ShapeDtypeStruct(w_out.shape, BF16)],
        scratch_shapes=[pltpu.VMEM((MAX_WINDOW, tn), F32)],
        compiler_params=_params(("arbitrary", "arbitrary"), 58),
        name="b_in",
    )(h, w_in, w_in, wg2d, w_out)


def _b_out_kernel(pooled_ref, wg_ref, bg_ref, sc_ref, sz_ref, wo_ref, x_ref, pg_ref,
                  o_ref, y_buf, *, idx):
    def make_y(y_ref):
        mixed = _dot(pooled_ref[...], wg_ref[0]) + bg_ref[...]
        y_ref[...] = (mixed * sc_ref[...] * sz_ref[...].astype(F32)).astype(BF16)

    _out_projection_pipeline(idx, make_y, y_buf, wo_ref, x_ref, pg_ref, o_ref)


def _b_out(pooled, w_grp, b_grp, scale, sz, w_out, x2d, post_gain, tm=512, tn=512):
    m, e = pooled.shape
    d = w_out.shape[1]
    groups, gw, _ = w_grp.shape
    nn = gw // tn
    idx = _PipelineIndex(m // tm, groups * nn)
    made_rc = lambda t: idx.made(t)
    made_col = lambda t: (0, idx.made(t)[1])
    proj_row = lambda t: (idx.projected(t)[0], 0)
    return pl.pallas_call(
        functools.partial(_b_out_kernel, idx=idx),
        grid=(idx.total + 1,),
        in_specs=[pl.BlockSpec((tm, gw), lambda t: (idx.made(t)[0], idx.made(t)[1] // nn)),
                  pl.BlockSpec((1, gw, tn),
                               lambda t: (idx.made(t)[1] // nn, 0, idx.made(t)[1] % nn)),
                  pl.BlockSpec((1, tn), made_col),
                  pl.BlockSpec((1, tn), made_col),
                  pl.BlockSpec((tm, tn), made_rc),
                  pl.BlockSpec((tn, d), lambda t: (idx.projected(t)[1], 0)),
                  pl.BlockSpec((tm, d), proj_row),
                  pl.BlockSpec((1, d), lambda t: (0, 0))],
        out_specs=pl.BlockSpec((tm, d), proj_row),
        out_shape=jax.ShapeDtypeStruct((m, d), F32),
        scratch_shapes=[pltpu.VMEM((2, tm, tn), BF16)],
        compiler_params=_params(("arbitrary",), 56),
        name="b_out",
    )(pooled, w_grp, b_grp.reshape(1, e), scale.reshape(1, e), sz, w_out, x2d,
      post_gain.reshape(1, d))


def kernel(x, pre_norm, post_norm, a_w_in, a_ln_g, a_ln_b, a_w_s, a_b_s, a_w_out,
           b_w_in, b_w_grp, b_b_grp, b_scale, b_w_out):
    bsz, seq, d = x.shape
    e = a_w_out.shape[1]
    x2d = x.reshape(bsz * seq, d)

    h = _prenorm(x2d, pre_norm[0])
    a, g, s1, s2, a_wo, b_wi = _a_in(h, a_w_in[0], a_w_out[0], b_w_in[0], e)
    x2d, h = _a_out(a, g, s1, s2, a_ln_g[0], a_ln_b[0], a_w_s[0], a_b_s[0],
                    a_wo, x2d, post_norm[0], pre_norm[1])

    pooled, sz, b_wg, b_wo = _b_in(h, b_wi, b_w_grp[0], b_w_out[0], e, seq)
    x2d = _b_out(pooled, b_wg.reshape(b_w_grp[0].shape), b_b_grp[0], b_scale[0], sz,
                 b_wo, x2d, post_norm[1])
    return x2d.reshape(bsz, seq, d)
```

```python
import functools
import math

import jax
import jax.numpy as jnp
from jax import lax
from jax.experimental import pallas as pl
from jax.experimental.pallas import tpu as pltpu

EPS = 1e-6
CHUNK = 128
POOL_WINDOWS = (2, 4, 8, 16)
MAX_WINDOW = max(POOL_WINDOWS)
LANES = 128
MIB = 1024 * 1024

F32 = jnp.float32
BF16 = jnp.bfloat16


def _params(semantics, vmem_mib):
    return pltpu.CompilerParams(dimension_semantics=semantics,
                                vmem_limit_bytes=vmem_mib * MIB)


def _gelu(x):
    return 0.5 * x * (1.0 + lax.erf(x * math.sqrt(0.5)))


def _silu(x):
    return x / (1.0 + jnp.exp(-x))


def _dot(a, b):
    return jnp.dot(a, b, preferred_element_type=F32)


def _prenorm_kernel(x_ref, g_ref, h_ref):
    x = x_ref[...]
    ms = jnp.mean(x * x, axis=-1, keepdims=True)
    h = (x * lax.rsqrt(ms + EPS) * g_ref[...]).astype(BF16)
    h_ref[...] = pltpu.bitcast(h, jnp.uint32)


def _prenorm(x2d, gain, tm=512):
    m, d = x2d.shape
    return pl.pallas_call(
        _prenorm_kernel,
        grid=(m // tm,),
        in_specs=[pl.BlockSpec((tm, d), lambda i: (i, 0)),
                  pl.BlockSpec((1, d), lambda i: (0, 0))],
        out_specs=pl.BlockSpec((tm // 2, d), lambda i: (i, 0)),
        out_shape=jax.ShapeDtypeStruct((m // 2, d), jnp.uint32),
        compiler_params=_params(("parallel",), 40),
        name="prenorm",
    )(x2d, gain.reshape(1, d))


def _a_in_kernel(h_ref, wu_ref, wv_ref, wz_ref, wof_ref, wnf_ref, a_ref, g_ref, s1_ref,
                 s2_ref, wob_ref, wnb_ref):
    j = pl.program_id(1)
    wob_ref[...] = wof_ref[...].astype(BF16)
    wnb_ref[...] = wnf_ref[...].astype(BF16)
    h = pltpu.bitcast(h_ref[...], BF16)
    u = _dot(h, wu_ref[...].astype(BF16))
    v = _dot(h, wv_ref[...].astype(BF16))
    z = _dot(h, wz_ref[...].astype(BF16))
    a_ref[...] = (_gelu(u) * _silu(z)).astype(a_ref.dtype)
    gv = _gelu(v)
    g_ref[...] = gv.astype(g_ref.dtype)
    tn = gv.shape[1]
    p1 = gv[:, 0:LANES]
    p2 = p1 * p1
    for k in range(1, tn // LANES):
        blk = gv[:, k * LANES:(k + 1) * LANES]
        p1 = p1 + blk
        p2 = p2 + blk * blk

    @pl.when(j == 0)
    def _():
        s1_ref[...] = p1
        s2_ref[...] = p2

    @pl.when(j > 0)
    def _():
        s1_ref[...] += p1
        s2_ref[...] += p2


def _a_in(h, w_in, w_out, w_next, e, tm=1024, tn=256):
    m, d = 2 * h.shape[0], h.shape[1]
    nj = e // tn
    ni = m // tm
    slab_o = w_out.shape[0] // (ni * nj)
    slab_n = w_next.shape[0] // (ni * nj)
    row = lambda i, j: (i, 0)
    step = lambda i, j: (i * nj + j, 0)
    return pl.pallas_call(
        _a_in_kernel,
        grid=(ni, nj),
        in_specs=[pl.BlockSpec((tm // 2, d), row),
                  pl.BlockSpec((d, tn), lambda i, j: (0, j)),
                  pl.BlockSpec((d, tn), lambda i, j: (0, nj + j)),
                  pl.BlockSpec((d, tn), lambda i, j: (0, 2 * nj + j)),
                  pl.BlockSpec((slab_o, w_out.shape[1]), step),
                  pl.BlockSpec((slab_n, w_next.shape[1]), step)],
        out_specs=[pl.BlockSpec((tm, tn), lambda i, j: (i, j)),
                   pl.BlockSpec((tm, tn), lambda i, j: (i, j)),
                   pl.BlockSpec((tm, LANES), row),
                   pl.BlockSpec((tm, LANES), row),
                   pl.BlockSpec((slab_o, w_out.shape[1]), step),
                   pl.BlockSpec((slab_n, w_next.shape[1]), step)],
        out_shape=[jax.ShapeDtypeStruct((m, e), BF16),
                   jax.ShapeDtypeStruct((m, e), BF16),
                   jax.ShapeDtypeStruct((m, LANES), F32),
                   jax.ShapeDtypeStruct((m, LANES), F32),
                   jax.ShapeDtypeStruct(w_out.shape, BF16),
                   jax.ShapeDtypeStruct(w_next.shape, BF16)],
        compiler_params=_params(("parallel", "arbitrary"), 56),
        name="a_in",
    )(h, w_in, w_in, w_in, w_out, w_next)


OUT_COL_SPLIT = 4
EPILOGUE_ROWS = 128


class _PipelineIndex:
    def __init__(self, n_rows, n_blocks):
        self.n_blocks = n_blocks
        self.total = n_rows * n_blocks

    def made(self, t):
        tt = jnp.minimum(t, self.total - 1)
        return tt // self.n_blocks, tt % self.n_blocks

    def projected(self, t):
        tt = jnp.maximum(t - 1, 0)
        return tt // self.n_blocks, tt % self.n_blocks


X_SEM, O_SEM, H_SEM = 0, 1, 2


def _out_projection_pipeline(idx, make_y, y_buf, wo_ref, x_hbm, pg_ref, o_hbm, acc_ref,
                             xbuf, obuf, sems, next_gain_ref=None, h_hbm=None, hbuf=None):
    t = pl.program_id(0)
    tm, d = acc_ref.shape
    dn = d // OUT_COL_SPLIT
    n_chunks = tm // EPILOGUE_ROWS
    half = EPILOGUE_ROWS // 2
    slot = t % 2
    row_p, blk_p = idx.projected(t)
    row_tile_done = jnp.logical_and(t > 0, blk_p == idx.n_blocks - 1)

    def chunk_rows(c, packed=False):
        size = half if packed else EPILOGUE_ROWS
        return pl.ds(pl.multiple_of((row_p * n_chunks + c) * size, size), size)

    def x_copy(c):
        return pltpu.make_async_copy(x_hbm.at[chunk_rows(c), :], xbuf.at[c % 2],
                                     sems.at[X_SEM, c % 2])

    def o_copy(c):
        return pltpu.make_async_copy(obuf.at[c % 2], o_hbm.at[chunk_rows(c), :],
                                     sems.at[O_SEM, c % 2])

    def h_copy(c):
        return pltpu.make_async_copy(hbuf.at[c % 2], h_hbm.at[chunk_rows(c, packed=True), :],
                                     sems.at[H_SEM, c % 2])

    def project():
        y = y_buf[1 - slot]
        for nb in range(OUT_COL_SPLIT):
            cols = slice(nb * dn, (nb + 1) * dn)
            acc_ref[:, cols] += _dot(y, wo_ref[:, cols])

    @pl.when(row_tile_done)
    def _():
        x_copy(0).start()
        x_copy(1).start()

    @pl.when(jnp.logical_and(t > 0, blk_p == 0))
    def _():
        acc_ref[...] = jnp.zeros(acc_ref.shape, acc_ref.dtype)

    @pl.when(t == 0)
    def _():
        make_y(y_buf.at[slot])

    @pl.when(jnp.logical_and(t > 0, t < idx.total))
    def _():
        project()
        make_y(y_buf.at[slot])

    @pl.when(t == idx.total)
    def _():
        project()

    @pl.when(row_tile_done)
    def _():
        assert n_chunks >= 2
        for c in range(n_chunks):
            x_copy(c).wait()
            if c >= 2:
                o_copy(c - 2).wait()
                if h_hbm is not None:
                    h_copy(c - 2).wait()
            mm = acc_ref[c * EPILOGUE_ROWS:(c + 1) * EPILOGUE_ROWS, :]
            ms = jnp.mean(mm * mm, axis=-1, keepdims=True)
            xn = xbuf[c % 2] + mm * lax.rsqrt(ms + EPS) * pg_ref[...]
            obuf[c % 2] = xn
            if c + 2 < n_chunks:
                x_copy(c + 2).start()
            o_copy(c).start()
            if h_hbm is not None:
                ms_next = jnp.mean(xn * xn, axis=-1, keepdims=True)
                hn = (xn * lax.rsqrt(ms_next + EPS) * next_gain_ref[...]).astype(BF16)
                hbuf[c % 2] = pltpu.bitcast(hn, jnp.uint32)
                h_copy(c).start()
        for c in range(n_chunks - 2, n_chunks):
            o_copy(c).wait()
            if h_hbm is not None:
                h_copy(c).wait()


def _a_out_kernel(a_ref, g_ref, s1_ref, s2_ref, lng_ref, lnb_ref, ws_ref, bs_ref,
                  wo_ref, x_hbm, pg_ref, ng_ref, o_hbm, h_hbm, y_buf, acc_ref, xbuf, obuf,
                  hbuf, sems, *, e, idx):
    tm = a_ref.shape[0]

    def make_y(y_ref):
        mu = jnp.sum(s1_ref[...], axis=-1, keepdims=True) * (1.0 / e)
        var = jnp.sum(s2_ref[...], axis=-1, keepdims=True) * (1.0 / e) - mu * mu
        rstd = lax.rsqrt(var + EPS)
        g = g_ref[...].astype(F32)
        vn = ((g - mu) * rstd * lng_ref[...] + lnb_ref[...]).astype(BF16)
        w = ws_ref[0]
        t_idx = lax.broadcasted_iota(jnp.int32, w.shape, 0)
        s_idx = lax.broadcasted_iota(jnp.int32, w.shape, 1)
        wc = jnp.where(t_idx >= s_idx, w, 0.0).astype(BF16)
        bs = bs_ref[0]
        for c in range(tm // CHUNK):
            rows = slice(c * CHUNK, (c + 1) * CHUNK)
            sv = _dot(wc, vn[rows, :]) + bs
            y_ref[rows, :] = (a_ref[rows, :].astype(F32) * sv).astype(BF16)

    _out_projection_pipeline(idx, make_y, y_buf, wo_ref, x_hbm, pg_ref, o_hbm, acc_ref,
                             xbuf, obuf, sems, next_gain_ref=ng_ref, h_hbm=h_hbm, hbuf=hbuf)


def _a_out(a, g, s1, s2, ln_g, ln_b, w_s, b_s, w_out, x2d, post_gain, next_gain, tm=512):
    m, e = a.shape
    d = w_out.shape[1]
    heads = w_s.shape[0]
    hd = e // heads
    idx = _PipelineIndex(m // tm, heads)
    made_rc = lambda t: idx.made(t)
    made_row = lambda t: (idx.made(t)[0], 0)
    made_col = lambda t: (0, idx.made(t)[1])
    made_head = lambda t: (idx.made(t)[1], 0, 0)
    const = lambda t: (0, 0)
    in_hbm = pl.BlockSpec(memory_space=pl.ANY)
    return pl.pallas_call(
        functools.partial(_a_out_kernel, e=e, idx=idx),
        grid=(idx.total + 1,),
        in_specs=[pl.BlockSpec((tm, hd), made_rc),
                  pl.BlockSpec((tm, hd), made_rc),
                  pl.BlockSpec((tm, LANES), made_row),
                  pl.BlockSpec((tm, LANES), made_row),
                  pl.BlockSpec((1, hd), made_col),
                  pl.BlockSpec((1, hd), made_col),
                  pl.BlockSpec((1, CHUNK, CHUNK), made_head),
                  pl.BlockSpec((1, CHUNK, 1), made_head),
                  pl.BlockSpec((hd, d), lambda t: (idx.projected(t)[1], 0)),
                  in_hbm,
                  pl.BlockSpec((1, d), const),
                  pl.BlockSpec((1, d), const)],
        out_specs=[in_hbm, in_hbm],
        out_shape=[jax.ShapeDtypeStruct((m, d), F32),
                   jax.ShapeDtypeStruct((m // 2, d), jnp.uint32)],
        scratch_shapes=[pltpu.VMEM((2, tm, hd), BF16),
                        pltpu.VMEM((tm, d), F32),
                        pltpu.VMEM((2, EPILOGUE_ROWS, d), F32),
                        pltpu.VMEM((2, EPILOGUE_ROWS, d), F32),
                        pltpu.VMEM((2, EPILOGUE_ROWS // 2, d), jnp.uint32),
                        pltpu.SemaphoreType.DMA((3, 2))],
        compiler_params=_params(("arbitrary",), 56),
        name="a_out",
    )(a, g, s1, s2, ln_g.reshape(1, e), ln_b.reshape(1, e), w_s,
      b_s.reshape(heads, CHUNK, 1), w_out, x2d, post_gain.reshape(1, d),
      next_gain.reshape(1, d))


def _trailing_window_sums(ext, window):
    s = ext
    span = 1
    while span < window:
        s = s + pltpu.roll(s, span, 0)
        span *= 2
    assert span == window, "pool windows must be powers of two"
    return s


def _b_in_kernel(h_ref, wp_ref, wz_ref, wgf_ref, wof_ref, pooled_ref, sz_ref, wgb_ref,
                 wob_ref, carry_ref, *, seq, group_dim):
    j = pl.program_id(0)
    i = pl.program_id(1)
    tm, tn = pooled_ref.shape
    wgb_ref[...] = wgf_ref[...].astype(BF16)
    wob_ref[...] = wof_ref[...].astype(BF16)

    @pl.when(i == 0)
    def _():
        carry_ref[...] = jnp.zeros(carry_ref.shape, F32)

    h = pltpu.bitcast(h_ref[...], BF16)
    p = _dot(h, wp_ref[...])
    z = _dot(h, wz_ref[...])
    sz_ref[...] = _silu(z).astype(sz_ref.dtype)
    tile_in_seq = i % (seq // tm)
    halo = jnp.where(tile_in_seq == 0, 0.0, carry_ref[...])
    carry_ref[...] = p[tm - MAX_WINDOW:, :]
    ext = jnp.concatenate([halo, p], axis=0)
    head = slice(0, MAX_WINDOW)
    pos_head = tile_in_seq * tm + 1 + lax.broadcasted_iota(jnp.int32, (MAX_WINDOW, 1), 0)
    grp = j // (group_dim // tn)
    for gi, window in enumerate(POOL_WINDOWS):
        @pl.when(grp == gi)
        def _(window=window):
            sums = _trailing_window_sums(ext, window)[MAX_WINDOW:, :]
            pooled_ref[...] = (sums * (1.0 / window) - p).astype(pooled_ref.dtype)
            inv_cnt = 1.0 / jnp.minimum(pos_head, window).astype(F32)
            pooled_ref[head, :] = (sums[head, :] * inv_cnt - p[head, :]).astype(pooled_ref.dtype)


def _b_in(h, w_in, w_grp, w_out, e, seq, tm=1024, tn=512):
    m, d = 2 * h.shape[0], h.shape[1]
    nj = e // tn
    ni = m // tm
    group_dim = e // len(POOL_WINDOWS)
    wg2d = w_grp.reshape(-1, w_grp.shape[-1])
    slab_g = wg2d.shape[0] // (nj * ni)
    slab_o = w_out.shape[0] // (nj * ni)
    step = lambda j, i: (j * ni + i, 0)
    return pl.pallas_call(
        functools.partial(_b_in_kernel, seq=seq, group_dim=group_dim),
        grid=(nj, ni),
        in_specs=[pl.BlockSpec((tm // 2, d), lambda j, i: (i, 0)),
                  pl.BlockSpec((d, tn), lambda j, i: (0, j)),
                  pl.BlockSpec((d, tn), lambda j, i: (0, nj + j)),
                  pl.BlockSpec((slab_g, wg2d.shape[1]), step),
                  pl.BlockSpec((slab_o, w_out.shape[1]), step)],
        out_specs=[pl.BlockSpec((tm, tn), lambda j, i: (i, j)),
                   pl.BlockSpec((tm, tn), lambda j, i: (i, j)),
                   pl.BlockSpec((slab_g, wg2d.shape[1]), step),
                   pl.BlockSpec((slab_o, w_out.shape[1]), step)],
        out_shape=[jax.ShapeDtypeStruct((m, e), BF16),
                   jax.ShapeDtypeStruct((m, e), BF16),
                   jax.ShapeDtypeStruct(wg2d.shape, BF16),
                   jax.ShapeDtypeStruct(w_out.shape, BF16)],
        scratch_shapes=[pltpu.VMEM((MAX_WINDOW, tn), F32)],
        compiler_params=_params(("arbitrary", "arbitrary"), 58),
        name="b_in",
    )(h, w_in, w_in, wg2d, w_out)


def _b_out_kernel(pooled_ref, wg_ref, bg_ref, sc_ref, sz_ref, wo_ref, x_hbm, pg_ref,
                  o_hbm, y_buf, acc_ref, xbuf, obuf, sems, *, idx):
    def make_y(y_ref):
        mixed = _dot(pooled_ref[...], wg_ref[0]) + bg_ref[...]
        y_ref[...] = (mixed * sc_ref[...] * sz_ref[...].astype(F32)).astype(BF16)

    _out_projection_pipeline(idx, make_y, y_buf, wo_ref, x_hbm, pg_ref, o_hbm, acc_ref,
                             xbuf, obuf, sems)


def _b_out(pooled, w_grp, b_grp, scale, sz, w_out, x2d, post_gain, tm=512, tn=512):
    m, e = pooled.shape
    d = w_out.shape[1]
    groups, gw, _ = w_grp.shape
    nn = gw // tn
    idx = _PipelineIndex(m // tm, groups * nn)
    made_rc = lambda t: idx.made(t)
    made_col = lambda t: (0, idx.made(t)[1])
    in_hbm = pl.BlockSpec(memory_space=pl.ANY)
    return pl.pallas_call(
        functools.partial(_b_out_kernel, idx=idx),
        grid=(idx.total + 1,),
        in_specs=[pl.BlockSpec((tm, gw), lambda t: (idx.made(t)[0], idx.made(t)[1] // nn)),
                  pl.BlockSpec((1, gw, tn),
                               lambda t: (idx.made(t)[1] // nn, 0, idx.made(t)[1] % nn)),
                  pl.BlockSpec((1, tn), made_col),
                  pl.BlockSpec((1, tn), made_col),
                  pl.BlockSpec((tm, tn), made_rc),
                  pl.BlockSpec((tn, d), lambda t: (idx.projected(t)[1], 0)),
                  in_hbm,
                  pl.BlockSpec((1, d), lambda t: (0, 0))],
        out_specs=in_hbm,
        out_shape=jax.ShapeDtypeStruct((m, d), F32),
        scratch_shapes=[pltpu.VMEM((2, tm, tn), BF16),
                        pltpu.VMEM((tm, d), F32),
                        pltpu.VMEM((2, EPILOGUE_ROWS, d), F32),
                        pltpu.VMEM((2, EPILOGUE_ROWS, d), F32),
                        pltpu.SemaphoreType.DMA((3, 2))],
        compiler_params=_params(("arbitrary",), 56),
        name="b_out",
    )(pooled, w_grp, b_grp.reshape(1, e), scale.reshape(1, e), sz, w_out, x2d,
      post_gain.reshape(1, d))


def kernel(x, pre_norm, post_norm, a_w_in, a_ln_g, a_ln_b, a_w_s, a_b_s, a_w_out,
           b_w_in, b_w_grp, b_b_grp, b_scale, b_w_out):
    bsz, seq, d = x.shape
    e = a_w_out.shape[1]
    x2d = x.reshape(bsz * seq, d)

    h = _prenorm(x2d, pre_norm[0])
    a, g, s1, s2, a_wo, b_wi = _a_in(h, a_w_in[0], a_w_out[0], b_w_in[0], e)
    x2d, h = _a_out(a, g, s1, s2, a_ln_g[0], a_ln_b[0], a_w_s[0], a_b_s[0],
                    a_wo, x2d, post_norm[0], pre_norm[1])

    pooled, sz, b_wg, b_wo = _b_in(h, b_wi, b_w_grp[0], b_w_out[0], e, seq)
    x2d = _b_out(pooled, b_wg.reshape(b_w_grp[0].shape), b_b_grp[0], b_scale[0], sz,
                 b_wo, x2d, post_norm[1])
    return x2d.reshape(bsz, seq, d)
```

```python
import functools
import math

import jax
import jax.numpy as jnp
from jax import lax
from jax.experimental import pallas as pl
from jax.experimental.pallas import tpu as pltpu

EPS = 1e-6
CHUNK = 128
POOL_WINDOWS = (2, 4, 8, 16)
MAX_WINDOW = max(POOL_WINDOWS)
LANES = 128
MIB = 1024 * 1024

F32 = jnp.float32
BF16 = jnp.bfloat16


def _params(semantics, vmem_mib):
    return pltpu.CompilerParams(dimension_semantics=semantics,
                                vmem_limit_bytes=vmem_mib * MIB)


def _gelu(x):
    return 0.5 * x * (1.0 + lax.erf(x * math.sqrt(0.5)))


def _silu(x):
    return x / (1.0 + jnp.exp(-x))


def _dot(a, b):
    return jnp.dot(a, b, preferred_element_type=F32)


def _prenorm_kernel(x_ref, g_ref, h_ref):
    x = x_ref[...]
    ms = jnp.mean(x * x, axis=-1, keepdims=True)
    h = (x * lax.rsqrt(ms + EPS) * g_ref[...]).astype(BF16)
    h_ref[...] = pltpu.bitcast(h, jnp.uint32)


def _prenorm(x2d, gain, tm=512):
    m, d = x2d.shape
    return pl.pallas_call(
        _prenorm_kernel,
        grid=(m // tm,),
        in_specs=[pl.BlockSpec((tm, d), lambda i: (i, 0)),
                  pl.BlockSpec((1, d), lambda i: (0, 0))],
        out_specs=pl.BlockSpec((tm // 2, d), lambda i: (i, 0)),
        out_shape=jax.ShapeDtypeStruct((m // 2, d), jnp.uint32),
        compiler_params=_params(("parallel",), 40),
        name="prenorm",
    )(x2d, gain.reshape(1, d))


def _a_in_kernel(h_ref, wu_ref, wv_ref, wz_ref, wof_ref, wnf_ref, a_ref, g_ref, s1_ref,
                 s2_ref, wob_ref, wnb_ref):
    j = pl.program_id(1)
    wob_ref[...] = wof_ref[...].astype(BF16)
    wnb_ref[...] = wnf_ref[...].astype(BF16)
    h = pltpu.bitcast(h_ref[...], BF16)
    u = _dot(h, wu_ref[...].astype(BF16))
    v = _dot(h, wv_ref[...].astype(BF16))
    z = _dot(h, wz_ref[...].astype(BF16))
    a_ref[...] = (_gelu(u) * _silu(z)).astype(a_ref.dtype)
    gv = _gelu(v)
    g_ref[...] = gv.astype(g_ref.dtype)
    tn = gv.shape[1]
    p1 = gv[:, 0:LANES]
    p2 = p1 * p1
    for k in range(1, tn // LANES):
        blk = gv[:, k * LANES:(k + 1) * LANES]
        p1 = p1 + blk
        p2 = p2 + blk * blk

    @pl.when(j == 0)
    def _():
        s1_ref[...] = p1
        s2_ref[...] = p2

    @pl.when(j > 0)
    def _():
        s1_ref[...] += p1
        s2_ref[...] += p2


def _a_in(h, w_in, w_out, w_next, e, tm=1024, tn=256):
    m, d = 2 * h.shape[0], h.shape[1]
    nj = e // tn
    ni = m // tm
    slab_o = w_out.shape[0] // (ni * nj)
    slab_n = w_next.shape[0] // (ni * nj)
    row = lambda i, j: (i, 0)
    step = lambda i, j: (i * nj + j, 0)
    return pl.pallas_call(
        _a_in_kernel,
        grid=(ni, nj),
        in_specs=[pl.BlockSpec((tm // 2, d), row),
                  pl.BlockSpec((d, tn), lambda i, j: (0, j)),
                  pl.BlockSpec((d, tn), lambda i, j: (0, nj + j)),
                  pl.BlockSpec((d, tn), lambda i, j: (0, 2 * nj + j)),
                  pl.BlockSpec((slab_o, w_out.shape[1]), step),
                  pl.BlockSpec((slab_n, w_next.shape[1]), step)],
        out_specs=[pl.BlockSpec((tm, tn), lambda i, j: (i, j)),
                   pl.BlockSpec((tm, tn), lambda i, j: (i, j)),
                   pl.BlockSpec((tm, LANES), row),
                   pl.BlockSpec((tm, LANES), row),
                   pl.BlockSpec((slab_o, w_out.shape[1]), step),
                   pl.BlockSpec((slab_n, w_next.shape[1]), step)],
        out_shape=[jax.ShapeDtypeStruct((m, e), BF16),
                   jax.ShapeDtypeStruct((m, e), BF16),
                   jax.ShapeDtypeStruct((m, LANES), F32),
                   jax.ShapeDtypeStruct((m, LANES), F32),
                   jax.ShapeDtypeStruct(w_out.shape, BF16),
                   jax.ShapeDtypeStruct(w_next.shape, BF16)],
        compiler_params=_params(("parallel", "arbitrary"), 56),
        name="a_in",
    )(h, w_in, w_in, w_in, w_out, w_next)


OUT_COL_SPLIT = 4
EPILOGUE_ROWS = 128


class _PipelineIndex:
    def __init__(self, n_rows, n_blocks):
        self.n_blocks = n_blocks
        self.total = n_rows * n_blocks

    def made(self, t):
        tt = jnp.minimum(t, self.total - 1)
        return tt // self.n_blocks, tt % self.n_blocks

    def projected(self, t):
        tt = jnp.maximum(t - 1, 0)
        return tt // self.n_blocks, tt % self.n_blocks


X_SEM, O_SEM, H_SEM = 0, 1, 2


def _out_projection_pipeline(idx, make_y, y_buf, wo_ref, x_hbm, pg_ref, o_hbm, acc_ref,
                             xbuf, obuf, sems, next_gain_ref=None, h_hbm=None, hbuf=None):
    t = pl.program_id(0)
    tm, d = acc_ref.shape
    dn = d // OUT_COL_SPLIT
    n_chunks = tm // EPILOGUE_ROWS
    half = EPILOGUE_ROWS // 2
    slot = t % 2
    row_p, blk_p = idx.projected(t)
    row_tile_done = jnp.logical_and(t > 0, blk_p == idx.n_blocks - 1)

    def chunk_rows(c, packed=False):
        size = half if packed else EPILOGUE_ROWS
        return pl.ds(pl.multiple_of((row_p * n_chunks + c) * size, size), size)

    def x_copy():
        rows = pl.ds(pl.multiple_of(row_p * tm, tm), tm)
        return pltpu.make_async_copy(x_hbm.at[rows, :], xbuf, sems.at[X_SEM, 0])

    def o_copy(c):
        return pltpu.make_async_copy(obuf.at[c % 2], o_hbm.at[chunk_rows(c), :],
                                     sems.at[O_SEM, c % 2])

    def h_copy(c):
        return pltpu.make_async_copy(hbuf.at[c % 2], h_hbm.at[chunk_rows(c, packed=True), :],
                                     sems.at[H_SEM, c % 2])

    def wait_sent(c):
        o_copy(c).wait()
        if h_hbm is not None:
            h_copy(c).wait()

    def project():
        y = y_buf[1 - slot]
        for nb in range(OUT_COL_SPLIT):
            cols = slice(nb * dn, (nb + 1) * dn)
            acc_ref[:, cols] += _dot(y, wo_ref[:, cols])

    @pl.when(jnp.logical_and(t > 0, blk_p == 0))
    def _():
        x_copy().start()
        acc_ref[...] = jnp.zeros(acc_ref.shape, acc_ref.dtype)

    @pl.when(t == 0)
    def _():
        make_y(y_buf.at[slot])

    @pl.when(jnp.logical_and(t > 0, t < idx.total))
    def _():
        project()
        make_y(y_buf.at[slot])

    @pl.when(t == idx.total)
    def _():
        project()

    assert n_chunks >= 2 and n_chunks % 2 == 0

    @pl.when(jnp.logical_and(row_tile_done, row_p > 0))
    def _():
        wait_sent(0)
        wait_sent(1)

    @pl.when(row_tile_done)
    def _():
        x_copy().wait()
        for c in range(n_chunks):
            if c >= 2:
                wait_sent(c - 2)
            rows = slice(c * EPILOGUE_ROWS, (c + 1) * EPILOGUE_ROWS)
            mm = acc_ref[rows, :]
            ms = jnp.mean(mm * mm, axis=-1, keepdims=True)
            xn = xbuf[rows, :] + mm * lax.rsqrt(ms + EPS) * pg_ref[...]
            obuf[c % 2] = xn
            o_copy(c).start()
            if h_hbm is not None:
                ms_next = jnp.mean(xn * xn, axis=-1, keepdims=True)
                hn = (xn * lax.rsqrt(ms_next + EPS) * next_gain_ref[...]).astype(BF16)
                hbuf[c % 2] = pltpu.bitcast(hn, jnp.uint32)
                h_copy(c).start()

    @pl.when(t == idx.total)
    def _():
        wait_sent(n_chunks - 2)
        wait_sent(n_chunks - 1)


def _a_out_kernel(a_ref, g_ref, s1_ref, s2_ref, lng_ref, lnb_ref, ws_ref, bs_ref,
                  wo_ref, x_hbm, pg_ref, ng_ref, o_hbm, h_hbm, y_buf, acc_ref, xbuf, obuf,
                  hbuf, sems, *, e, idx):
    tm = a_ref.shape[0]

    def make_y(y_ref):
        mu = jnp.sum(s1_ref[...], axis=-1, keepdims=True) * (1.0 / e)
        var = jnp.sum(s2_ref[...], axis=-1, keepdims=True) * (1.0 / e) - mu * mu
        rstd = lax.rsqrt(var + EPS)
        g = g_ref[...].astype(F32)
        vn = ((g - mu) * rstd * lng_ref[...] + lnb_ref[...]).astype(BF16)
        w = ws_ref[0]
        t_idx = lax.broadcasted_iota(jnp.int32, w.shape, 0)
        s_idx = lax.broadcasted_iota(jnp.int32, w.shape, 1)
        wc = jnp.where(t_idx >= s_idx, w, 0.0).astype(BF16)
        bs = bs_ref[0]
        for c in range(tm // CHUNK):
            rows = slice(c * CHUNK, (c + 1) * CHUNK)
            sv = _dot(wc, vn[rows, :]) + bs
            y_ref[rows, :] = (a_ref[rows, :].astype(F32) * sv).astype(BF16)

    _out_projection_pipeline(idx, make_y, y_buf, wo_ref, x_hbm, pg_ref, o_hbm, acc_ref,
                             xbuf, obuf, sems, next_gain_ref=ng_ref, h_hbm=h_hbm, hbuf=hbuf)


def _a_out(a, g, s1, s2, ln_g, ln_b, w_s, b_s, w_out, x2d, post_gain, next_gain, tm=512):
    m, e = a.shape
    d = w_out.shape[1]
    heads = w_s.shape[0]
    hd = e // heads
    idx = _PipelineIndex(m // tm, heads)
    made_rc = lambda t: idx.made(t)
    made_row = lambda t: (idx.made(t)[0], 0)
    made_col = lambda t: (0, idx.made(t)[1])
    made_head = lambda t: (idx.made(t)[1], 0, 0)
    const = lambda t: (0, 0)
    in_hbm = pl.BlockSpec(memory_space=pl.ANY)
    return pl.pallas_call(
        functools.partial(_a_out_kernel, e=e, idx=idx),
        grid=(idx.total + 1,),
        in_specs=[pl.BlockSpec((tm, hd), made_rc),
                  pl.BlockSpec((tm, hd), made_rc),
                  pl.BlockSpec((tm, LANES), made_row),
                  pl.BlockSpec((tm, LANES), made_row),
                  pl.BlockSpec((1, hd), made_col),
                  pl.BlockSpec((1, hd), made_col),
                  pl.BlockSpec((1, CHUNK, CHUNK), made_head),
                  pl.BlockSpec((1, CHUNK, 1), made_head),
                  pl.BlockSpec((hd, d), lambda t: (idx.projected(t)[1], 0)),
                  in_hbm,
                  pl.BlockSpec((1, d), const),
                  pl.BlockSpec((1, d), const)],
        out_specs=[in_hbm, in_hbm],
        out_shape=[jax.ShapeDtypeStruct((m, d), F32),
                   jax.ShapeDtypeStruct((m // 2, d), jnp.uint32)],
        scratch_shapes=[pltpu.VMEM((2, tm, hd), BF16),
                        pltpu.VMEM((tm, d), F32),
                        pltpu.VMEM((tm, d), F32),
                        pltpu.VMEM((2, EPILOGUE_ROWS, d), F32),
                        pltpu.VMEM((2, EPILOGUE_ROWS // 2, d), jnp.uint32),
                        pltpu.SemaphoreType.DMA((3, 2))],
        compiler_params=_params(("arbitrary",), 56),
        name="a_out",
    )(a, g, s1, s2, ln_g.reshape(1, e), ln_b.reshape(1, e), w_s,
      b_s.reshape(heads, CHUNK, 1), w_out, x2d, post_gain.reshape(1, d),
      next_gain.reshape(1, d))


def _trailing_window_sums(ext, window):
    s = ext
    span = 1
    while span < window:
        s = s + pltpu.roll(s, span, 0)
        span *= 2
    assert span == window, "pool windows must be powers of two"
    return s


def _b_in_kernel(h_ref, wp_ref, wz_ref, wgf_ref, wof_ref, pooled_ref, sz_ref, wgb_ref,
                 wob_ref, carry_ref, *, seq, group_dim):
    j = pl.program_id(0)
    i = pl.program_id(1)
    tm, tn = pooled_ref.shape
    wgb_ref[...] = wgf_ref[...].astype(BF16)
    wob_ref[...] = wof_ref[...].astype(BF16)

    @pl.when(i == 0)
    def _():
        carry_ref[...] = jnp.zeros(carry_ref.shape, F32)

    h = pltpu.bitcast(h_ref[...], BF16)
    p = _dot(h, wp_ref[...])
    z = _dot(h, wz_ref[...])
    sz_ref[...] = _silu(z).astype(sz_ref.dtype)
    tile_in_seq = i % (seq // tm)
    halo = jnp.where(tile_in_seq == 0, 0.0, carry_ref[...])
    carry_ref[...] = p[tm - MAX_WINDOW:, :]
    ext = jnp.concatenate([halo, p], axis=0)
    head = slice(0, MAX_WINDOW)
    pos_head = tile_in_seq * tm + 1 + lax.broadcasted_iota(jnp.int32, (MAX_WINDOW, 1), 0)
    grp = j // (group_dim // tn)
    for gi, window in enumerate(POOL_WINDOWS):
        @pl.when(grp == gi)
        def _(window=window):
            sums = _trailing_window_sums(ext, window)[MAX_WINDOW:, :]
            pooled_ref[...] = (sums * (1.0 / window) - p).astype(pooled_ref.dtype)
            inv_cnt = 1.0 / jnp.minimum(pos_head, window).astype(F32)
            pooled_ref[head, :] = (sums[head, :] * inv_cnt - p[head, :]).astype(pooled_ref.dtype)


def _b_in(h, w_in, w_grp, w_out, e, seq, tm=1024, tn=512):
    m, d = 2 * h.shape[0], h.shape[1]
    nj = e // tn
    ni = m // tm
    group_dim = e // len(POOL_WINDOWS)
    wg2d = w_grp.reshape(-1, w_grp.shape[-1])
    slab_g = wg2d.shape[0] // (nj * ni)
    slab_o = w_out.shape[0] // (nj * ni)
    step = lambda j, i: (j * ni + i, 0)
    return pl.pallas_call(
        functools.partial(_b_in_kernel, seq=seq, group_dim=group_dim),
        grid=(nj, ni),
        in_specs=[pl.BlockSpec((tm // 2, d), lambda j, i: (i, 0)),
                  pl.BlockSpec((d, tn), lambda j, i: (0, j)),
                  pl.BlockSpec((d, tn), lambda j, i: (0, nj + j)),
                  pl.BlockSpec((slab_g, wg2d.shape[1]), step),
                  pl.BlockSpec((slab_o, w_out.shape[1]), step)],
        out_specs=[pl.BlockSpec((tm, tn), lambda j, i: (i, j)),
                   pl.BlockSpec((tm, tn), lambda j, i: (i, j)),
                   pl.BlockSpec((slab_g, wg2d.shape[1]), step),
                   pl.BlockSpec((slab_o, w_out.shape[1]), step)],
        out_shape=[jax.ShapeDtypeStruct((m, e), BF16),
                   jax.ShapeDtypeStruct((m, e), BF16),
                   jax.ShapeDtypeStruct(wg2d.shape, BF16),
                   jax.ShapeDtypeStruct(w_out.shape, BF16)],
        scratch_shapes=[pltpu.VMEM((MAX_WINDOW, tn), F32)],
        compiler_params=_params(("arbitrary", "arbitrary"), 58),
        name="b_in",
    )(h, w_in, w_in, wg2d, w_out)


def _b_out_kernel(pooled_ref, wg_ref, bg_ref, sc_ref, sz_ref, wo_ref, x_hbm, pg_ref,
                  o_hbm, y_buf, acc_ref, xbuf, obuf, sems, *, idx):
    def make_y(y_ref):
        mixed = _dot(pooled_ref[...], wg_ref[0]) + bg_ref[...]
        y_ref[...] = (mixed * sc_ref[...] * sz_ref[...].astype(F32)).astype(BF16)

    _out_projection_pipeline(idx, make_y, y_buf, wo_ref, x_hbm, pg_ref, o_hbm, acc_ref,
                             xbuf, obuf, sems)


def _b_out(pooled, w_grp, b_grp, scale, sz, w_out, x2d, post_gain, tm=512, tn=512):
    m, e = pooled.shape
    d = w_out.shape[1]
    groups, gw, _ = w_grp.shape
    nn = gw // tn
    idx = _PipelineIndex(m // tm, groups * nn)
    made_rc = lambda t: idx.made(t)
    made_col = lambda t: (0, idx.made(t)[1])
    in_hbm = pl.BlockSpec(memory_space=pl.ANY)
    return pl.pallas_call(
        functools.partial(_b_out_kernel, idx=idx),
        grid=(idx.total + 1,),
        in_specs=[pl.BlockSpec((tm, gw), lambda t: (idx.made(t)[0], idx.made(t)[1] // nn)),
                  pl.BlockSpec((1, gw, tn),
                               lambda t: (idx.made(t)[1] // nn, 0, idx.made(t)[1] % nn)),
                  pl.BlockSpec((1, tn), made_col),
                  pl.BlockSpec((1, tn), made_col),
                  pl.BlockSpec((tm, tn), made_rc),
                  pl.BlockSpec((tn, d), lambda t: (idx.projected(t)[1], 0)),
                  in_hbm,
                  pl.BlockSpec((1, d), lambda t: (0, 0))],
        out_specs=in_hbm,
        out_shape=jax.ShapeDtypeStruct((m, d), F32),
        scratch_shapes=[pltpu.VMEM((2, tm, tn), BF16),
                        pltpu.VMEM((tm, d), F32),
                        pltpu.VMEM((tm, d), F32),
                        pltpu.VMEM((2, EPILOGUE_ROWS, d), F32),
                        pltpu.SemaphoreType.DMA((3, 2))],
        compiler_params=_params(("arbitrary",), 56),
        name="b_out",
    )(pooled, w_grp, b_grp.reshape(1, e), scale.reshape(1, e), sz, w_out, x2d,
      post_gain.reshape(1, d))


def kernel(x, pre_norm, post_norm, a_w_in, a_ln_g, a_ln_b, a_w_s, a_b_s, a_w_out,
           b_w_in, b_w_grp, b_b_grp, b_scale, b_w_out):
    bsz, seq, d = x.shape
    e = a_w_out.shape[1]
    x2d = x.reshape(bsz * seq, d)

    h = _prenorm(x2d, pre_norm[0])
    a, g, s1, s2, a_wo, b_wi = _a_in(h, a_w_in[0], a_w_out[0], b_w_in[0], e)
    x2d, h = _a_out(a, g, s1, s2, a_ln_g[0], a_ln_b[0], a_w_s[0], a_b_s[0],
                    a_wo, x2d, post_norm[0], pre_norm[1])

    pooled, sz, b_wg, b_wo = _b_in(h, b_wi, b_w_grp[0], b_w_out[0], e, seq)
    x2d = _b_out(pooled, b_wg.reshape(b_w_grp[0].shape), b_b_grp[0], b_scale[0], sz,
                 b_wo, x2d, post_norm[1])
    return x2d.reshape(bsz, seq, d)
```

```python
import functools
import math

import jax
import jax.numpy as jnp
from jax import lax
from jax.experimental import pallas as pl
from jax.experimental.pallas import tpu as pltpu

EPS = 1e-6
CHUNK = 128
POOL_WINDOWS = (2, 4, 8, 16)
MAX_WINDOW = max(POOL_WINDOWS)
LANES = 128
MIB = 1024 * 1024

F32 = jnp.float32
BF16 = jnp.bfloat16


def _params(semantics, vmem_mib):
    return pltpu.CompilerParams(dimension_semantics=semantics,
                                vmem_limit_bytes=vmem_mib * MIB)


def _gelu(x):
    return 0.5 * x * (1.0 + lax.erf(x * math.sqrt(0.5)))


def _silu(x):
    return x / (1.0 + jnp.exp(-x))


def _dot(a, b):
    return jnp.dot(a, b, preferred_element_type=F32)


def _prenorm_kernel(x_ref, g_ref, h_ref):
    x = x_ref[...]
    ms = jnp.mean(x * x, axis=-1, keepdims=True)
    h = (x * lax.rsqrt(ms + EPS) * g_ref[...]).astype(BF16)
    h_ref[...] = pltpu.bitcast(h, jnp.uint32)


def _prenorm(x2d, gain, tm=512):
    m, d = x2d.shape
    return pl.pallas_call(
        _prenorm_kernel,
        grid=(m // tm,),
        in_specs=[pl.BlockSpec((tm, d), lambda i: (i, 0)),
                  pl.BlockSpec((1, d), lambda i: (0, 0))],
        out_specs=pl.BlockSpec((tm // 2, d), lambda i: (i, 0)),
        out_shape=jax.ShapeDtypeStruct((m // 2, d), jnp.uint32),
        compiler_params=_params(("parallel",), 40),
        name="prenorm",
    )(x2d, gain.reshape(1, d))


def _a_in_kernel(h_ref, wu_ref, wv_ref, wz_ref, wof_ref, wnf_ref, a_ref, g_ref, s1_ref,
                 s2_ref, wob_ref, wnb_ref):
    j = pl.program_id(1)
    wob_ref[...] = wof_ref[...].astype(BF16)
    wnb_ref[...] = wnf_ref[...].astype(BF16)
    h = pltpu.bitcast(h_ref[...], BF16)
    u = _dot(h, wu_ref[...].astype(BF16))
    v = _dot(h, wv_ref[...].astype(BF16))
    z = _dot(h, wz_ref[...].astype(BF16))
    a_ref[...] = (_gelu(u) * _silu(z)).astype(a_ref.dtype)
    gv = _gelu(v)
    g_ref[...] = gv.astype(g_ref.dtype)
    tn = gv.shape[1]
    p1 = gv[:, 0:LANES]
    p2 = p1 * p1
    for k in range(1, tn // LANES):
        blk = gv[:, k * LANES:(k + 1) * LANES]
        p1 = p1 + blk
        p2 = p2 + blk * blk

    @pl.when(j == 0)
    def _():
        s1_ref[...] = p1
        s2_ref[...] = p2

    @pl.when(j > 0)
    def _():
        s1_ref[...] += p1
        s2_ref[...] += p2


def _a_in(h, w_in, w_out, w_next, e, tm=1024, tn=256):
    m, d = 2 * h.shape[0], h.shape[1]
    nj = e // tn
    ni = m // tm
    slab_o = w_out.shape[0] // (ni * nj)
    slab_n = w_next.shape[0] // (ni * nj)
    row = lambda i, j: (i, 0)
    step = lambda i, j: (i * nj + j, 0)
    return pl.pallas_call(
        _a_in_kernel,
        grid=(ni, nj),
        in_specs=[pl.BlockSpec((tm // 2, d), row),
                  pl.BlockSpec((d, tn), lambda i, j: (0, j)),
                  pl.BlockSpec((d, tn), lambda i, j: (0, nj + j)),
                  pl.BlockSpec((d, tn), lambda i, j: (0, 2 * nj + j)),
                  pl.BlockSpec((slab_o, w_out.shape[1]), step),
                  pl.BlockSpec((slab_n, w_next.shape[1]), step)],
        out_specs=[pl.BlockSpec((tm, tn), lambda i, j: (i, j)),
                   pl.BlockSpec((tm, tn), lambda i, j: (i, j)),
                   pl.BlockSpec((tm, LANES), row),
                   pl.BlockSpec((tm, LANES), row),
                   pl.BlockSpec((slab_o, w_out.shape[1]), step),
                   pl.BlockSpec((slab_n, w_next.shape[1]), step)],
        out_shape=[jax.ShapeDtypeStruct((m, e), BF16),
                   jax.ShapeDtypeStruct((m, e), BF16),
                   jax.ShapeDtypeStruct((m, LANES), F32),
                   jax.ShapeDtypeStruct((m, LANES), F32),
                   jax.ShapeDtypeStruct(w_out.shape, BF16),
                   jax.ShapeDtypeStruct(w_next.shape, BF16)],
        compiler_params=_params(("parallel", "arbitrary"), 56),
        name="a_in",
    )(h, w_in, w_in, w_in, w_out, w_next)


OUT_COL_SPLIT = 4
EPILOGUE_ROWS = 128


class _PipelineIndex:
    def __init__(self, n_rows, n_blocks):
        self.n_blocks = n_blocks
        self.total = n_rows * n_blocks

    def made(self, t):
        tt = jnp.minimum(t, self.total - 1)
        return tt // self.n_blocks, tt % self.n_blocks

    def projected(self, t):
        tt = jnp.maximum(t - 1, 0)
        return tt // self.n_blocks, tt % self.n_blocks


X_SEM, O_SEM, H_SEM = 0, 1, 2


def _out_projection_pipeline(idx, make_y, y_buf, wo_ref, x_hbm, pg_ref, o_hbm, acc_ref,
                             xbuf, obuf, sems, next_gain_ref=None, h_hbm=None, hbuf=None):
    t = pl.program_id(0)
    tm, d = acc_ref.shape
    dn = d // OUT_COL_SPLIT
    n_chunks = tm // EPILOGUE_ROWS
    half = EPILOGUE_ROWS // 2
    slot = t % 2
    row_p, blk_p = idx.projected(t)
    row_tile_done = jnp.logical_and(t > 0, blk_p == idx.n_blocks - 1)

    def chunk_rows(c, packed=False):
        size = half if packed else EPILOGUE_ROWS
        return pl.ds(pl.multiple_of((row_p * n_chunks + c) * size, size), size)

    def x_copy():
        rows = pl.ds(pl.multiple_of(row_p * tm, tm), tm)
        return pltpu.make_async_copy(x_hbm.at[rows, :], xbuf, sems.at[X_SEM, 0])

    def o_copy(c):
        return pltpu.make_async_copy(obuf.at[c % 2], o_hbm.at[chunk_rows(c), :],
                                     sems.at[O_SEM, c % 2])

    def h_copy(c):
        return pltpu.make_async_copy(hbuf.at[c % 2], h_hbm.at[chunk_rows(c, packed=True), :],
                                     sems.at[H_SEM, c % 2])

    def wait_sent(c):
        o_copy(c).wait()
        if h_hbm is not None:
            h_copy(c).wait()

    def project():
        y = y_buf[1 - slot]
        for nb in range(OUT_COL_SPLIT):
            cols = slice(nb * dn, (nb + 1) * dn)
            acc_ref[:, cols] += _dot(y, wo_ref[:, cols])

    @pl.when(jnp.logical_and(t > 0, blk_p == 0))
    def _():
        x_copy().start()
        acc_ref[...] = jnp.zeros(acc_ref.shape, acc_ref.dtype)

    @pl.when(t == 0)
    def _():
        make_y(y_buf.at[slot])

    @pl.when(jnp.logical_and(t > 0, t < idx.total))
    def _():
        project()
        make_y(y_buf.at[slot])

    @pl.when(t == idx.total)
    def _():
        project()

    assert n_chunks >= 2 and n_chunks % 2 == 0

    @pl.when(jnp.logical_and(row_tile_done, row_p > 0))
    def _():
        wait_sent(0)
        wait_sent(1)

    @pl.when(row_tile_done)
    def _():
        x_copy().wait()
        for c in range(n_chunks):
            if c >= 2:
                wait_sent(c - 2)
            rows = slice(c * EPILOGUE_ROWS, (c + 1) * EPILOGUE_ROWS)
            mm = acc_ref[rows, :]
            ms = jnp.mean(mm * mm, axis=-1, keepdims=True)
            xn = xbuf[rows, :] + mm * lax.rsqrt(ms + EPS) * pg_ref[...]
            obuf[c % 2] = xn
            o_copy(c).start()
            if h_hbm is not None:
                ms_next = jnp.mean(xn * xn, axis=-1, keepdims=True)
                hn = (xn * lax.rsqrt(ms_next + EPS) * next_gain_ref[...]).astype(BF16)
                hbuf[c % 2] = pltpu.bitcast(hn, jnp.uint32)
                h_copy(c).start()

    @pl.when(t == idx.total)
    def _():
        wait_sent(n_chunks - 2)
        wait_sent(n_chunks - 1)


def _a_out_kernel(a_ref, g_ref, s1_ref, s2_ref, lng_ref, lnb_ref, ws_ref, bs_ref,
                  wo_ref, x_hbm, pg_ref, ng_ref, o_hbm, h_hbm, y_buf, acc_ref, xbuf, obuf,
                  hbuf, sems, *, e, idx):
    tm = a_ref.shape[0]

    def make_y(y_ref):
        mu = jnp.sum(s1_ref[...], axis=-1, keepdims=True) * (1.0 / e)
        var = jnp.sum(s2_ref[...], axis=-1, keepdims=True) * (1.0 / e) - mu * mu
        rstd = lax.rsqrt(var + EPS)
        g = g_ref[...].astype(F32)
        vn = ((g - mu) * rstd * lng_ref[...] + lnb_ref[...]).astype(BF16)
        heads_per_block = ws_ref.shape[0]
        hd = g_ref.shape[1] // heads_per_block
        t_idx = lax.broadcasted_iota(jnp.int32, (CHUNK, CHUNK), 0)
        s_idx = lax.broadcasted_iota(jnp.int32, (CHUNK, CHUNK), 1)
        for hh in range(heads_per_block):
            cols = slice(hh * hd, (hh + 1) * hd)
            wc = jnp.where(t_idx >= s_idx, ws_ref[hh], 0.0).astype(BF16)
            bs = bs_ref[hh]
            for c in range(tm // CHUNK):
                rows = slice(c * CHUNK, (c + 1) * CHUNK)
                sv = _dot(wc, vn[rows, cols]) + bs
                y_ref[rows, cols] = (a_ref[rows, cols].astype(F32) * sv).astype(BF16)

    _out_projection_pipeline(idx, make_y, y_buf, wo_ref, x_hbm, pg_ref, o_hbm, acc_ref,
                             xbuf, obuf, sems, next_gain_ref=ng_ref, h_hbm=h_hbm, hbuf=hbuf)


def _a_out(a, g, s1, s2, ln_g, ln_b, w_s, b_s, w_out, x2d, post_gain, next_gain, tm=512,
           heads_per_block=2):
    m, e = a.shape
    d = w_out.shape[1]
    heads = w_s.shape[0]
    hd = (e // heads) * heads_per_block
    idx = _PipelineIndex(m // tm, heads // heads_per_block)
    made_rc = lambda t: idx.made(t)
    made_row = lambda t: (idx.made(t)[0], 0)
    made_col = lambda t: (0, idx.made(t)[1])
    made_head = lambda t: (idx.made(t)[1], 0, 0)
    const = lambda t: (0, 0)
    in_hbm = pl.BlockSpec(memory_space=pl.ANY)
    return pl.pallas_call(
        functools.partial(_a_out_kernel, e=e, idx=idx),
        grid=(idx.total + 1,),
        in_specs=[pl.BlockSpec((tm, hd), made_rc),
                  pl.BlockSpec((tm, hd), made_rc),
                  pl.BlockSpec((tm, LANES), made_row),
                  pl.BlockSpec((tm, LANES), made_row),
                  pl.BlockSpec((1, hd), made_col),
                  pl.BlockSpec((1, hd), made_col),
                  pl.BlockSpec((heads_per_block, CHUNK, CHUNK), made_head),
                  pl.BlockSpec((heads_per_block, CHUNK, 1), made_head),
                  pl.BlockSpec((hd, d), lambda t: (idx.projected(t)[1], 0)),
                  in_hbm,
                  pl.BlockSpec((1, d), const),
                  pl.BlockSpec((1, d), const)],
        out_specs=[in_hbm, in_hbm],
        out_shape=[jax.ShapeDtypeStruct((m, d), F32),
                   jax.ShapeDtypeStruct((m // 2, d), jnp.uint32)],
        scratch_shapes=[pltpu.VMEM((2, tm, hd), BF16),
                        pltpu.VMEM((tm, d), F32),
                        pltpu.VMEM((tm, d), F32),
                        pltpu.VMEM((2, EPILOGUE_ROWS, d), F32),
                        pltpu.VMEM((2, EPILOGUE_ROWS // 2, d), jnp.uint32),
                        pltpu.SemaphoreType.DMA((3, 2))],
        compiler_params=_params(("arbitrary",), 56),
        name="a_out",
    )(a, g, s1, s2, ln_g.reshape(1, e), ln_b.reshape(1, e), w_s,
      b_s.reshape(heads, CHUNK, 1), w_out, x2d, post_gain.reshape(1, d),
      next_gain.reshape(1, d))


def _trailing_window_sums(ext, window):
    s = ext
    span = 1
    while span < window:
        s = s + pltpu.roll(s, span, 0)
        span *= 2
    assert span == window, "pool windows must be powers of two"
    return s


def _b_in_kernel(h_ref, wp_ref, wz_ref, wgf_ref, wof_ref, pooled_ref, sz_ref, wgb_ref,
                 wob_ref, carry_ref, *, seq, group_dim):
    j = pl.program_id(0)
    i = pl.program_id(1)
    tm, tn = pooled_ref.shape
    wgb_ref[...] = wgf_ref[...].astype(BF16)
    wob_ref[...] = wof_ref[...].astype(BF16)

    @pl.when(i == 0)
    def _():
        carry_ref[...] = jnp.zeros(carry_ref.shape, F32)

    h = pltpu.bitcast(h_ref[...], BF16)
    p = _dot(h, wp_ref[...])
    z = _dot(h, wz_ref[...])
    sz_ref[...] = _silu(z).astype(sz_ref.dtype)
    tile_in_seq = i % (seq // tm)
    halo = jnp.where(tile_in_seq == 0, 0.0, carry_ref[...])
    carry_ref[...] = p[tm - MAX_WINDOW:, :]
    ext = jnp.concatenate([halo, p], axis=0)
    head = slice(0, MAX_WINDOW)
    pos_head = tile_in_seq * tm + 1 + lax.broadcasted_iota(jnp.int32, (MAX_WINDOW, 1), 0)
    grp = j // (group_dim // tn)
    for gi, window in enumerate(POOL_WINDOWS):
        @pl.when(grp == gi)
        def _(window=window):
            sums = _trailing_window_sums(ext, window)[MAX_WINDOW:, :]
            pooled_ref[...] = (sums * (1.0 / window) - p).astype(pooled_ref.dtype)
            inv_cnt = 1.0 / jnp.minimum(pos_head, window).astype(F32)
            pooled_ref[head, :] = (sums[head, :] * inv_cnt - p[head, :]).astype(pooled_ref.dtype)


def _b_in(h, w_in, w_grp, w_out, e, seq, tm=1024, tn=512):
    m, d = 2 * h.shape[0], h.shape[1]
    nj = e // tn
    ni = m // tm
    group_dim = e // len(POOL_WINDOWS)
    wg2d = w_grp.reshape(-1, w_grp.shape[-1])
    slab_g = wg2d.shape[0] // (nj * ni)
    slab_o = w_out.shape[0] // (nj * ni)
    step = lambda j, i: (j * ni + i, 0)
    return pl.pallas_call(
        functools.partial(_b_in_kernel, seq=seq, group_dim=group_dim),
        grid=(nj, ni),
        in_specs=[pl.BlockSpec((tm // 2, d), lambda j, i: (i, 0)),
                  pl.BlockSpec((d, tn), lambda j, i: (0, j)),
                  pl.BlockSpec((d, tn), lambda j, i: (0, nj + j)),
                  pl.BlockSpec((slab_g, wg2d.shape[1]), step),
                  pl.BlockSpec((slab_o, w_out.shape[1]), step)],
        out_specs=[pl.BlockSpec((tm, tn), lambda j, i: (i, j)),
                   pl.BlockSpec((tm, tn), lambda j, i: (i, j)),
                   pl.BlockSpec((slab_g, wg2d.shape[1]), step),
                   pl.BlockSpec((slab_o, w_out.shape[1]), step)],
        out_shape=[jax.ShapeDtypeStruct((m, e), BF16),
                   jax.ShapeDtypeStruct((m, e), BF16),
                   jax.ShapeDtypeStruct(wg2d.shape, BF16),
                   jax.ShapeDtypeStruct(w_out.shape, BF16)],
        scratch_shapes=[pltpu.VMEM((MAX_WINDOW, tn), F32)],
        compiler_params=_params(("arbitrary", "arbitrary"), 58),
        name="b_in",
    )(h, w_in, w_in, wg2d, w_out)


def _b_out_kernel(pooled_ref, wg_ref, bg_ref, sc_ref, sz_ref, wo_ref, x_hbm, pg_ref,
                  o_hbm, y_buf, acc_ref, xbuf, obuf, sems, *, idx):
    def make_y(y_ref):
        mixed = _dot(pooled_ref[...], wg_ref[0]) + bg_ref[...]
        y_ref[...] = (mixed * sc_ref[...] * sz_ref[...].astype(F32)).astype(BF16)

    _out_projection_pipeline(idx, make_y, y_buf, wo_ref, x_hbm, pg_ref, o_hbm, acc_ref,
                             xbuf, obuf, sems)


def _b_out(pooled, w_grp, b_grp, scale, sz, w_out, x2d, post_gain, tm=512, tn=1024):
    m, e = pooled.shape
    d = w_out.shape[1]
    groups, gw, _ = w_grp.shape
    nn = gw // tn
    idx = _PipelineIndex(m // tm, groups * nn)
    made_rc = lambda t: idx.made(t)
    made_col = lambda t: (0, idx.made(t)[1])
    in_hbm = pl.BlockSpec(memory_space=pl.ANY)
    return pl.pallas_call(
        functools.partial(_b_out_kernel, idx=idx),
        grid=(idx.total + 1,),
        in_specs=[pl.BlockSpec((tm, gw), lambda t: (idx.made(t)[0], idx.made(t)[1] // nn)),
                  pl.BlockSpec((1, gw, tn),
                               lambda t: (idx.made(t)[1] // nn, 0, idx.made(t)[1] % nn)),
                  pl.BlockSpec((1, tn), made_col),
                  pl.BlockSpec((1, tn), made_col),
                  pl.BlockSpec((tm, tn), made_rc),
                  pl.BlockSpec((tn, d), lambda t: (idx.projected(t)[1], 0)),
                  in_hbm,
                  pl.BlockSpec((1, d), lambda t: (0, 0))],
        out_specs=in_hbm,
        out_shape=jax.ShapeDtypeStruct((m, d), F32),
        scratch_shapes=[pltpu.VMEM((2, tm, tn), BF16),
                        pltpu.VMEM((tm, d), F32),
                        pltpu.VMEM((tm, d), F32),
                        pltpu.VMEM((2, EPILOGUE_ROWS, d), F32),
                        pltpu.SemaphoreType.DMA((3, 2))],
        compiler_params=_params(("arbitrary",), 56),
        name="b_out",
    )(pooled, w_grp, b_grp.reshape(1, e), scale.reshape(1, e), sz, w_out, x2d,
      post_gain.reshape(1, d))


def kernel(x, pre_norm, post_norm, a_w_in, a_ln_g, a_ln_b, a_w_s, a_b_s, a_w_out,
           b_w_in, b_w_grp, b_b_grp, b_scale, b_w_out):
    bsz, seq, d = x.shape
    e = a_w_out.shape[1]
    x2d = x.reshape(bsz * seq, d)

    h = _prenorm(x2d, pre_norm[0])
    a, g, s1, s2, a_wo, b_wi = _a_in(h, a_w_in[0], a_w_out[0], b_w_in[0], e)
    x2d, h = _a_out(a, g, s1, s2, a_ln_g[0], a_ln_b[0], a_w_s[0], a_b_s[0],
                    a_wo, x2d, post_norm[0], pre_norm[1])

    pooled, sz, b_wg, b_wo = _b_in(h, b_wi, b_w_grp[0], b_w_out[0], e, seq)
    x2d = _b_out(pooled, b_wg.reshape(b_w_grp[0].shape), b_b_grp[0], b_scale[0], sz,
                 b_wo, x2d, post_norm[1])
    return x2d.reshape(bsz, seq, d)
```

```python
import functools
import math

import jax
import jax.numpy as jnp
from jax import lax
from jax.experimental import pallas as pl
from jax.experimental.pallas import tpu as pltpu

EPS = 1e-6
CHUNK = 128
POOL_WINDOWS = (2, 4, 8, 16)
MAX_WINDOW = max(POOL_WINDOWS)
LANES = 128
MIB = 1024 * 1024

F32 = jnp.float32
BF16 = jnp.bfloat16


def _params(semantics, vmem_mib):
    return pltpu.CompilerParams(dimension_semantics=semantics,
                                vmem_limit_bytes=vmem_mib * MIB)


def _gelu(x):
    return 0.5 * x * (1.0 + lax.erf(x * math.sqrt(0.5)))


def _silu(x):
    return x / (1.0 + jnp.exp(-x))


def _dot(a, b):
    return jnp.dot(a, b, preferred_element_type=F32)


def _prenorm_kernel(x_ref, g_ref, h_ref):
    x = x_ref[...]
    ms = jnp.mean(x * x, axis=-1, keepdims=True)
    h = (x * lax.rsqrt(ms + EPS) * g_ref[...]).astype(BF16)
    h_ref[...] = pltpu.bitcast(h, jnp.uint32)


def _prenorm(x2d, gain, tm=512):
    m, d = x2d.shape
    return pl.pallas_call(
        _prenorm_kernel,
        grid=(m // tm,),
        in_specs=[pl.BlockSpec((tm, d), lambda i: (i, 0)),
                  pl.BlockSpec((1, d), lambda i: (0, 0))],
        out_specs=pl.BlockSpec((tm // 2, d), lambda i: (i, 0)),
        out_shape=jax.ShapeDtypeStruct((m // 2, d), jnp.uint32),
        compiler_params=_params(("parallel",), 40),
        name="prenorm",
    )(x2d, gain.reshape(1, d))


def _a_in_kernel(h_ref, wu_ref, wv_ref, wz_ref, wof_ref, wnf_ref, a_ref, g_ref, s1_ref,
                 s2_ref, wob_ref, wnb_ref):
    j = pl.program_id(1)
    wob_ref[...] = wof_ref[...].astype(BF16)
    wnb_ref[...] = wnf_ref[...].astype(BF16)
    h = pltpu.bitcast(h_ref[...], BF16)
    u = _dot(h, wu_ref[...].astype(BF16))
    v = _dot(h, wv_ref[...].astype(BF16))
    z = _dot(h, wz_ref[...].astype(BF16))
    a_ref[...] = (_gelu(u) * _silu(z)).astype(a_ref.dtype)
    gv = _gelu(v)
    g_ref[...] = gv.astype(g_ref.dtype)
    tn = gv.shape[1]
    p1 = gv[:, 0:LANES]
    p2 = p1 * p1
    for k in range(1, tn // LANES):
        blk = gv[:, k * LANES:(k + 1) * LANES]
        p1 = p1 + blk
        p2 = p2 + blk * blk

    @pl.when(j == 0)
    def _():
        s1_ref[...] = p1
        s2_ref[...] = p2

    @pl.when(j > 0)
    def _():
        s1_ref[...] += p1
        s2_ref[...] += p2


def _a_in(h, w_in, w_out, w_next, e, tm=1024, tn=256):
    m, d = 2 * h.shape[0], h.shape[1]
    nj = e // tn
    ni = m // tm
    slab_o = w_out.shape[0] // (ni * nj)
    slab_n = w_next.shape[0] // (ni * nj)
    row = lambda i, j: (i, 0)
    step = lambda i, j: (i * nj + j, 0)
    return pl.pallas_call(
        _a_in_kernel,
        grid=(ni, nj),
        in_specs=[pl.BlockSpec((tm // 2, d), row),
                  pl.BlockSpec((d, tn), lambda i, j: (0, j)),
                  pl.BlockSpec((d, tn), lambda i, j: (0, nj + j)),
                  pl.BlockSpec((d, tn), lambda i, j: (0, 2 * nj + j)),
                  pl.BlockSpec((slab_o, w_out.shape[1]), step),
                  pl.BlockSpec((slab_n, w_next.shape[1]), step)],
        out_specs=[pl.BlockSpec((tm, tn), lambda i, j: (i, j)),
                   pl.BlockSpec((tm, tn), lambda i, j: (i, j)),
                   pl.BlockSpec((tm, LANES), row),
                   pl.BlockSpec((tm, LANES), row),
                   pl.BlockSpec((slab_o, w_out.shape[1]), step),
                   pl.BlockSpec((slab_n, w_next.shape[1]), step)],
        out_shape=[jax.ShapeDtypeStruct((m, e), BF16),
                   jax.ShapeDtypeStruct((m, e), BF16),
                   jax.ShapeDtypeStruct((m, LANES), F32),
                   jax.ShapeDtypeStruct((m, LANES), F32),
                   jax.ShapeDtypeStruct(w_out.shape, BF16),
                   jax.ShapeDtypeStruct(w_next.shape, BF16)],
        compiler_params=_params(("parallel", "arbitrary"), 56),
        name="a_in",
    )(h, w_in, w_in, w_in, w_out, w_next)


OUT_COL_SPLIT = 4
EPILOGUE_ROWS = 128


class _PipelineIndex:
    def __init__(self, n_rows, n_blocks):
        self.n_blocks = n_blocks
        self.total = n_rows * n_blocks

    def made(self, t):
        tt = jnp.minimum(t, self.total - 1)
        return tt // self.n_blocks, tt % self.n_blocks

    def projected(self, t):
        tt = jnp.maximum(t - 1, 0)
        return tt // self.n_blocks, tt % self.n_blocks


X_SEM, O_SEM, H_SEM = 0, 1, 2


def _out_projection_pipeline(idx, make_y, y_buf, wo_ref, x_hbm, pg_ref, o_hbm, acc_ref,
                             xbuf, obuf, sems, next_gain_ref=None, h_hbm=None, hbuf=None):
    t = pl.program_id(0)
    tm, d = acc_ref.shape
    dn = d // OUT_COL_SPLIT
    n_chunks = tm // EPILOGUE_ROWS
    half = EPILOGUE_ROWS // 2
    slot = t % 2
    row_p, blk_p = idx.projected(t)
    row_tile_done = jnp.logical_and(t > 0, blk_p == idx.n_blocks - 1)

    def chunk_rows(c, packed=False):
        size = half if packed else EPILOGUE_ROWS
        return pl.ds(pl.multiple_of((row_p * n_chunks + c) * size, size), size)

    def x_copy():
        rows = pl.ds(pl.multiple_of(row_p * tm, tm), tm)
        return pltpu.make_async_copy(x_hbm.at[rows, :], xbuf, sems.at[X_SEM, 0])

    def o_copy(c):
        return pltpu.make_async_copy(obuf.at[c % 2], o_hbm.at[chunk_rows(c), :],
                                     sems.at[O_SEM, c % 2])

    def h_copy(c):
        return pltpu.make_async_copy(hbuf.at[c % 2], h_hbm.at[chunk_rows(c, packed=True), :],
                                     sems.at[H_SEM, c % 2])

    def wait_sent(c):
        o_copy(c).wait()
        if h_hbm is not None:
            h_copy(c).wait()

    def project():
        y = y_buf[1 - slot]
        for nb in range(OUT_COL_SPLIT):
            cols = slice(nb * dn, (nb + 1) * dn)
            acc_ref[:, cols] += _dot(y, wo_ref[:, cols])

    @pl.when(jnp.logical_and(t > 0, blk_p == 0))
    def _():
        x_copy().start()
        acc_ref[...] = jnp.zeros(acc_ref.shape, acc_ref.dtype)

    @pl.when(t == 0)
    def _():
        make_y(y_buf.at[slot])

    @pl.when(jnp.logical_and(t > 0, t < idx.total))
    def _():
        project()
        make_y(y_buf.at[slot])

    @pl.when(t == idx.total)
    def _():
        project()

    assert n_chunks >= 2 and n_chunks % 2 == 0

    @pl.when(jnp.logical_and(row_tile_done, row_p > 0))
    def _():
        wait_sent(0)
        wait_sent(1)

    @pl.when(row_tile_done)
    def _():
        x_copy().wait()
        for c in range(n_chunks):
            if c >= 2:
                wait_sent(c - 2)
            rows = slice(c * EPILOGUE_ROWS, (c + 1) * EPILOGUE_ROWS)
            mm = acc_ref[rows, :]
            ms = jnp.mean(mm * mm, axis=-1, keepdims=True)
            xn = xbuf[rows, :] + mm * lax.rsqrt(ms + EPS) * pg_ref[...]
            obuf[c % 2] = xn
            o_copy(c).start()
            if h_hbm is not None:
                ms_next = jnp.mean(xn * xn, axis=-1, keepdims=True)
                hn = (xn * lax.rsqrt(ms_next + EPS) * next_gain_ref[...]).astype(BF16)
                hbuf[c % 2] = pltpu.bitcast(hn, jnp.uint32)
                h_copy(c).start()

    @pl.when(t == idx.total)
    def _():
        wait_sent(n_chunks - 2)
        wait_sent(n_chunks - 1)


def _a_out_kernel(a_ref, g_ref, s1_ref, s2_ref, lng_ref, lnb_ref, ws_ref, bs_ref,
                  wo_ref, x_hbm, pg_ref, ng_ref, o_hbm, h_hbm, y_buf, acc_ref, xbuf, obuf,
                  hbuf, sems, *, e, idx):
    tm = a_ref.shape[0]

    def make_y(y_ref):
        mu = jnp.sum(s1_ref[...], axis=-1, keepdims=True) * (1.0 / e)
        var = jnp.sum(s2_ref[...], axis=-1, keepdims=True) * (1.0 / e) - mu * mu
        rstd = lax.rsqrt(jnp.maximum(var, 0.0) + EPS)
        g = g_ref[...].astype(F32)
        vn = ((g - mu) * rstd * lng_ref[...] + lnb_ref[...]).astype(BF16)
        heads_per_block = ws_ref.shape[0]
        hd = g_ref.shape[1] // heads_per_block
        t_idx = lax.broadcasted_iota(jnp.int32, (CHUNK, CHUNK), 0)
        s_idx = lax.broadcasted_iota(jnp.int32, (CHUNK, CHUNK), 1)
        for hh in range(heads_per_block):
            cols = slice(hh * hd, (hh + 1) * hd)
            wc = jnp.where(t_idx >= s_idx, ws_ref[hh], 0.0).astype(BF16)
            bs = bs_ref[hh]
            for c in range(tm // CHUNK):
                rows = slice(c * CHUNK, (c + 1) * CHUNK)
                sv = _dot(wc, vn[rows, cols]) + bs
                y_ref[rows, cols] = (a_ref[rows, cols].astype(F32) * sv).astype(BF16)

    _out_projection_pipeline(idx, make_y, y_buf, wo_ref, x_hbm, pg_ref, o_hbm, acc_ref,
                             xbuf, obuf, sems, next_gain_ref=ng_ref, h_hbm=h_hbm, hbuf=hbuf)


def _a_out(a, g, s1, s2, ln_g, ln_b, w_s, b_s, w_out, x2d, post_gain, next_gain, tm=512,
           heads_per_block=2):
    m, e = a.shape
    d = w_out.shape[1]
    heads = w_s.shape[0]
    hd = (e // heads) * heads_per_block
    idx = _PipelineIndex(m // tm, heads // heads_per_block)
    made_rc = lambda t: idx.made(t)
    made_row = lambda t: (idx.made(t)[0], 0)
    made_col = lambda t: (0, idx.made(t)[1])
    made_head = lambda t: (idx.made(t)[1], 0, 0)
    const = lambda t: (0, 0)
    in_hbm = pl.BlockSpec(memory_space=pl.ANY)
    return pl.pallas_call(
        functools.partial(_a_out_kernel, e=e, idx=idx),
        grid=(idx.total + 1,),
        in_specs=[pl.BlockSpec((tm, hd), made_rc),
                  pl.BlockSpec((tm, hd), made_rc),
                  pl.BlockSpec((tm, LANES), made_row),
                  pl.BlockSpec((tm, LANES), made_row),
                  pl.BlockSpec((1, hd), made_col),
                  pl.BlockSpec((1, hd), made_col),
                  pl.BlockSpec((heads_per_block, CHUNK, CHUNK), made_head),
                  pl.BlockSpec((heads_per_block, CHUNK, 1), made_head),
                  pl.BlockSpec((hd, d), lambda t: (idx.projected(t)[1], 0)),
                  in_hbm,
                  pl.BlockSpec((1, d), const),
                  pl.BlockSpec((1, d), const)],
        out_specs=[in_hbm, in_hbm],
        out_shape=[jax.ShapeDtypeStruct((m, d), F32),
                   jax.ShapeDtypeStruct((m // 2, d), jnp.uint32)],
        scratch_shapes=[pltpu.VMEM((2, tm, hd), BF16),
                        pltpu.VMEM((tm, d), F32),
                        pltpu.VMEM((tm, d), F32),
                        pltpu.VMEM((2, EPILOGUE_ROWS, d), F32),
                        pltpu.VMEM((2, EPILOGUE_ROWS // 2, d), jnp.uint32),
                        pltpu.SemaphoreType.DMA((3, 2))],
        compiler_params=_params(("arbitrary",), 56),
        name="a_out",
    )(a, g, s1, s2, ln_g.reshape(1, e), ln_b.reshape(1, e), w_s,
      b_s.reshape(heads, CHUNK, 1), w_out, x2d, post_gain.reshape(1, d),
      next_gain.reshape(1, d))


def _trailing_window_sums(ext, window):
    s = ext
    span = 1
    while span < window:
        s = s + pltpu.roll(s, span, 0)
        span *= 2
    assert span == window, "pool windows must be powers of two"
    return s


def _b_in_kernel(h_ref, wp_ref, wz_ref, wgf_ref, wof_ref, pooled_ref, sz_ref, wgb_ref,
                 wob_ref, carry_ref, *, seq, group_dim):
    j = pl.program_id(0)
    i = pl.program_id(1)
    tm, tn = pooled_ref.shape
    wgb_ref[...] = wgf_ref[...].astype(BF16)
    wob_ref[...] = wof_ref[...].astype(BF16)

    @pl.when(i == 0)
    def _():
        carry_ref[...] = jnp.zeros(carry_ref.shape, F32)

    h = pltpu.bitcast(h_ref[...], BF16)
    p = _dot(h, wp_ref[...])
    z = _dot(h, wz_ref[...])
    sz_ref[...] = _silu(z).astype(sz_ref.dtype)
    tile_in_seq = i % (seq // tm)
    halo = jnp.where(tile_in_seq == 0, 0.0, carry_ref[...])
    carry_ref[...] = p[tm - MAX_WINDOW:, :]
    ext = jnp.concatenate([halo, p], axis=0)
    head = slice(0, MAX_WINDOW)
    pos_head = tile_in_seq * tm + 1 + lax.broadcasted_iota(jnp.int32, (MAX_WINDOW, 1), 0)
    grp = j // (group_dim // tn)
    for gi, window in enumerate(POOL_WINDOWS):
        @pl.when(grp == gi)
        def _(window=window):
            sums = _trailing_window_sums(ext, window)[MAX_WINDOW:, :]
            pooled_ref[...] = (sums * (1.0 / window) - p).astype(pooled_ref.dtype)
            inv_cnt = 1.0 / jnp.minimum(pos_head, window).astype(F32)
            pooled_ref[head, :] = (sums[head, :] * inv_cnt - p[head, :]).astype(pooled_ref.dtype)


def _b_in(h, w_in, w_grp, w_out, e, seq, tm=1024, tn=512):
    m, d = 2 * h.shape[0], h.shape[1]
    nj = e // tn
    ni = m // tm
    group_dim = e // len(POOL_WINDOWS)
    wg2d = w_grp.reshape(-1, w_grp.shape[-1])
    slab_g = wg2d.shape[0] // (nj * ni)
    slab_o = w_out.shape[0] // (nj * ni)
    step = lambda j, i: (j * ni + i, 0)
    return pl.pallas_call(
        functools.partial(_b_in_kernel, seq=seq, group_dim=group_dim),
        grid=(nj, ni),
        in_specs=[pl.BlockSpec((tm // 2, d), lambda j, i: (i, 0)),
                  pl.BlockSpec((d, tn), lambda j, i: (0, j)),
                  pl.BlockSpec((d, tn), lambda j, i: (0, nj + j)),
                  pl.BlockSpec((slab_g, wg2d.shape[1]), step),
                  pl.BlockSpec((slab_o, w_out.shape[1]), step)],
        out_specs=[pl.BlockSpec((tm, tn), lambda j, i: (i, j)),
                   pl.BlockSpec((tm, tn), lambda j, i: (i, j)),
                   pl.BlockSpec((slab_g, wg2d.shape[1]), step),
                   pl.BlockSpec((slab_o, w_out.shape[1]), step)],
        out_shape=[jax.ShapeDtypeStruct((m, e), BF16),
                   jax.ShapeDtypeStruct((m, e), BF16),
                   jax.ShapeDtypeStruct(wg2d.shape, BF16),
                   jax.ShapeDtypeStruct(w_out.shape, BF16)],
        scratch_shapes=[pltpu.VMEM((MAX_WINDOW, tn), F32)],
        compiler_params=_params(("arbitrary", "arbitrary"), 58),
        name="b_in",
    )(h, w_in, w_in, wg2d, w_out)


def _b_out_kernel(pooled_ref, wg_ref, bg_ref, sc_ref, sz_ref, wo_ref, x_hbm, pg_ref,
                  o_hbm, y_buf, acc_ref, xbuf, obuf, sems, *, idx):
    def make_y(y_ref):
        mixed = _dot(pooled_ref[...], wg_ref[0]) + bg_ref[...]
        y_ref[...] = (mixed * sc_ref[...] * sz_ref[...].astype(F32)).astype(BF16)

    _out_projection_pipeline(idx, make_y, y_buf, wo_ref, x_hbm, pg_ref, o_hbm, acc_ref,
                             xbuf, obuf, sems)


def _b_out(pooled, w_grp, b_grp, scale, sz, w_out, x2d, post_gain, tm=512, tn=1024):
    m, e = pooled.shape
    d = w_out.shape[1]
    groups, gw, _ = w_grp.shape
    nn = gw // tn
    idx = _PipelineIndex(m // tm, groups * nn)
    made_rc = lambda t: idx.made(t)
    made_col = lambda t: (0, idx.made(t)[1])
    in_hbm = pl.BlockSpec(memory_space=pl.ANY)
    return pl.pallas_call(
        functools.partial(_b_out_kernel, idx=idx),
        grid=(idx.total + 1,),
        in_specs=[pl.BlockSpec((tm, gw), lambda t: (idx.made(t)[0], idx.made(t)[1] // nn)),
                  pl.BlockSpec((1, gw, tn),
                               lambda t: (idx.made(t)[1] // nn, 0, idx.made(t)[1] % nn)),
                  pl.BlockSpec((1, tn), made_col),
                  pl.BlockSpec((1, tn), made_col),
                  pl.BlockSpec((tm, tn), made_rc),
                  pl.BlockSpec((tn, d), lambda t: (idx.projected(t)[1], 0)),
                  in_hbm,
                  pl.BlockSpec((1, d), lambda t: (0, 0))],
        out_specs=in_hbm,
        out_shape=jax.ShapeDtypeStruct((m, d), F32),
        scratch_shapes=[pltpu.VMEM((2, tm, tn), BF16),
                        pltpu.VMEM((tm, d), F32),
                        pltpu.VMEM((tm, d), F32),
                        pltpu.VMEM((2, EPILOGUE_ROWS, d), F32),
                        pltpu.SemaphoreType.DMA((3, 2))],
        compiler_params=_params(("arbitrary",), 56),
        name="b_out",
    )(pooled, w_grp, b_grp.reshape(1, e), scale.reshape(1, e), sz, w_out, x2d,
      post_gain.reshape(1, d))


def kernel(x, pre_norm, post_norm, a_w_in, a_ln_g, a_ln_b, a_w_s, a_b_s, a_w_out,
           b_w_in, b_w_grp, b_b_grp, b_scale, b_w_out):
    bsz, seq, d = x.shape
    e = a_w_out.shape[1]
    x2d = x.reshape(bsz * seq, d)

    h = _prenorm(x2d, pre_norm[0])
    a, g, s1, s2, a_wo, b_wi = _a_in(h, a_w_in[0], a_w_out[0], b_w_in[0], e)
    x2d, h = _a_out(a, g, s1, s2, a_ln_g[0], a_ln_b[0], a_w_s[0], a_b_s[0],
                    a_wo, x2d, post_norm[0], pre_norm[1])

    pooled, sz, b_wg, b_wo = _b_in(h, b_wi, b_w_grp[0], b_w_out[0], e, seq)
    x2d = _b_out(pooled, b_wg.reshape(b_w_grp[0].shape), b_b_grp[0], b_scale[0], sz,
                 b_wo, x2d, post_norm[1])
    return x2d.reshape(bsz, seq, d)
```

```python
import functools
import math

import jax
import jax.numpy as jnp
from jax import lax
from jax.experimental import pallas as pl
from jax.experimental.pallas import tpu as pltpu

EPS = 1e-6
CHUNK = 128
POOL_WINDOWS = (2, 4, 8, 16)
MAX_WINDOW = max(POOL_WINDOWS)
LANES = 128
MIB = 1024 * 1024

F32 = jnp.float32
BF16 = jnp.bfloat16


def _params(semantics, vmem_mib):
    return pltpu.CompilerParams(dimension_semantics=semantics,
                                vmem_limit_bytes=vmem_mib * MIB)


def _gelu(x):
    return 0.5 * x * (1.0 + lax.erf(x * math.sqrt(0.5)))


def _silu(x):
    return x / (1.0 + jnp.exp(-x))


def _dot(a, b):
    return jnp.dot(a, b, preferred_element_type=F32)


def _prenorm_kernel(x_ref, g_ref, h_ref):
    x = x_ref[...]
    ms = jnp.mean(x * x, axis=-1, keepdims=True)
    h = (x * lax.rsqrt(ms + EPS) * g_ref[...]).astype(BF16)
    h_ref[...] = pltpu.bitcast(h, jnp.uint32)


def _prenorm(x2d, gain, tm=512):
    m, d = x2d.shape
    return pl.pallas_call(
        _prenorm_kernel,
        grid=(m // tm,),
        in_specs=[pl.BlockSpec((tm, d), lambda i: (i, 0)),
                  pl.BlockSpec((1, d), lambda i: (0, 0))],
        out_specs=pl.BlockSpec((tm // 2, d), lambda i: (i, 0)),
        out_shape=jax.ShapeDtypeStruct((m // 2, d), jnp.uint32),
        compiler_params=_params(("parallel",), 40),
        name="prenorm",
    )(x2d, gain.reshape(1, d))


ROW_CHUNKS = 4


def _a_in_kernel(h_ref, wu_ref, wv_ref, wz_ref, wof_ref, wnf_ref, a_ref, g_ref, s1_ref,
                 s2_ref, wob_ref, wnb_ref):
    j = pl.program_id(1)
    wob_ref[...] = wof_ref[...].astype(BF16)
    wnb_ref[...] = wnf_ref[...].astype(BF16)
    tm, tn = a_ref.shape
    wu = wu_ref[...].astype(BF16)
    wv = wv_ref[...].astype(BF16)
    wz = wz_ref[...].astype(BF16)
    n_row_chunks = ROW_CHUNKS
    rc = tm // n_row_chunks
    p1s, p2s = [], []
    for r in range(n_row_chunks):
        rows = slice(r * rc, (r + 1) * rc)
        h = pltpu.bitcast(h_ref[r * rc // 2:(r + 1) * rc // 2, :], BF16)
        u = _dot(h, wu)
        v = _dot(h, wv)
        z = _dot(h, wz)
        a_ref[rows, :] = (_gelu(u) * _silu(z)).astype(a_ref.dtype)
        gv = _gelu(v)
        g_ref[rows, :] = gv.astype(g_ref.dtype)
        q1 = gv[:, 0:LANES]
        q2 = q1 * q1
        for k in range(1, tn // LANES):
            blk = gv[:, k * LANES:(k + 1) * LANES]
            q1 = q1 + blk
            q2 = q2 + blk * blk
        p1s.append(q1)
        p2s.append(q2)
    p1 = jnp.concatenate(p1s, axis=0)
    p2 = jnp.concatenate(p2s, axis=0)

    @pl.when(j == 0)
    def _():
        s1_ref[...] = p1
        s2_ref[...] = p2

    @pl.when(j > 0)
    def _():
        s1_ref[...] += p1
        s2_ref[...] += p2


def _a_in(h, w_in, w_out, w_next, e, tm=1024, tn=256):
    m, d = 2 * h.shape[0], h.shape[1]
    nj = e // tn
    ni = m // tm
    slab_o = w_out.shape[0] // (ni * nj)
    slab_n = w_next.shape[0] // (ni * nj)
    row = lambda i, j: (i, 0)
    step = lambda i, j: (i * nj + j, 0)
    return pl.pallas_call(
        _a_in_kernel,
        grid=(ni, nj),
        in_specs=[pl.BlockSpec((tm // 2, d), row),
                  pl.BlockSpec((d, tn), lambda i, j: (0, j)),
                  pl.BlockSpec((d, tn), lambda i, j: (0, nj + j)),
                  pl.BlockSpec((d, tn), lambda i, j: (0, 2 * nj + j)),
                  pl.BlockSpec((slab_o, w_out.shape[1]), step),
                  pl.BlockSpec((slab_n, w_next.shape[1]), step)],
        out_specs=[pl.BlockSpec((tm, tn), lambda i, j: (i, j)),
                   pl.BlockSpec((tm, tn), lambda i, j: (i, j)),
                   pl.BlockSpec((tm, LANES), row),
                   pl.BlockSpec((tm, LANES), row),
                   pl.BlockSpec((slab_o, w_out.shape[1]), step),
                   pl.BlockSpec((slab_n, w_next.shape[1]), step)],
        out_shape=[jax.ShapeDtypeStruct((m, e), BF16),
                   jax.ShapeDtypeStruct((m, e), BF16),
                   jax.ShapeDtypeStruct((m, LANES), F32),
                   jax.ShapeDtypeStruct((m, LANES), F32),
                   jax.ShapeDtypeStruct(w_out.shape, BF16),
                   jax.ShapeDtypeStruct(w_next.shape, BF16)],
        compiler_params=_params(("parallel", "arbitrary"), 58),
        name="a_in",
    )(h, w_in, w_in, w_in, w_out, w_next)


OUT_COL_SPLIT = 4
EPILOGUE_ROWS = 128


class _PipelineIndex:
    def __init__(self, n_rows, n_blocks):
        self.n_blocks = n_blocks
        self.total = n_rows * n_blocks

    def made(self, t):
        tt = jnp.minimum(t, self.total - 1)
        return tt // self.n_blocks, tt % self.n_blocks

    def projected(self, t):
        tt = jnp.maximum(t - 1, 0)
        return tt // self.n_blocks, tt % self.n_blocks


X_SEM, O_SEM, H_SEM = 0, 1, 2


def _out_projection_pipeline(idx, make_y, y_buf, wo_ref, x_hbm, pg_ref, o_hbm, acc_ref,
                             xbuf, obuf, sems, next_gain_ref=None, h_hbm=None, hbuf=None):
    t = pl.program_id(0)
    tm, d = acc_ref.shape
    dn = d // OUT_COL_SPLIT
    n_chunks = tm // EPILOGUE_ROWS
    half = EPILOGUE_ROWS // 2
    slot = t % 2
    row_p, blk_p = idx.projected(t)
    row_tile_done = jnp.logical_and(t > 0, blk_p == idx.n_blocks - 1)

    def chunk_rows(c, packed=False):
        size = half if packed else EPILOGUE_ROWS
        return pl.ds(pl.multiple_of((row_p * n_chunks + c) * size, size), size)

    def x_copy():
        rows = pl.ds(pl.multiple_of(row_p * tm, tm), tm)
        return pltpu.make_async_copy(x_hbm.at[rows, :], xbuf, sems.at[X_SEM, 0])

    def o_copy(c):
        return pltpu.make_async_copy(obuf.at[c % 2], o_hbm.at[chunk_rows(c), :],
                                     sems.at[O_SEM, c % 2])

    def h_copy(c):
        return pltpu.make_async_copy(hbuf.at[c % 2], h_hbm.at[chunk_rows(c, packed=True), :],
                                     sems.at[H_SEM, c % 2])

    def wait_sent(c):
        o_copy(c).wait()
        if h_hbm is not None:
            h_copy(c).wait()

    def project():
        y = y_buf[1 - slot]
        for nb in range(OUT_COL_SPLIT):
            cols = slice(nb * dn, (nb + 1) * dn)
            acc_ref[:, cols] += _dot(y, wo_ref[:, cols])

    @pl.when(jnp.logical_and(t > 0, blk_p == 0))
    def _():
        x_copy().start()
        acc_ref[...] = jnp.zeros(acc_ref.shape, acc_ref.dtype)

    @pl.when(t == 0)
    def _():
        make_y(y_buf.at[slot])

    @pl.when(jnp.logical_and(t > 0, t < idx.total))
    def _():
        project()
        make_y(y_buf.at[slot])

    @pl.when(t == idx.total)
    def _():
        project()

    assert n_chunks >= 2 and n_chunks % 2 == 0

    @pl.when(jnp.logical_and(row_tile_done, row_p > 0))
    def _():
        wait_sent(0)
        wait_sent(1)

    @pl.when(row_tile_done)
    def _():
        x_copy().wait()
        for c in range(n_chunks):
            if c >= 2:
                wait_sent(c - 2)
            rows = slice(c * EPILOGUE_ROWS, (c + 1) * EPILOGUE_ROWS)
            mm = acc_ref[rows, :]
            ms = jnp.mean(mm * mm, axis=-1, keepdims=True)
            xn = xbuf[rows, :] + mm * lax.rsqrt(ms + EPS) * pg_ref[...]
            obuf[c % 2] = xn
            o_copy(c).start()
            if h_hbm is not None:
                ms_next = jnp.mean(xn * xn, axis=-1, keepdims=True)
                hn = (xn * lax.rsqrt(ms_next + EPS) * next_gain_ref[...]).astype(BF16)
                hbuf[c % 2] = pltpu.bitcast(hn, jnp.uint32)
                h_copy(c).start()

    @pl.when(t == idx.total)
    def _():
        wait_sent(n_chunks - 2)
        wait_sent(n_chunks - 1)


def _a_out_kernel(a_ref, g_ref, s1_ref, s2_ref, lng_ref, lnb_ref, ws_ref, bs_ref,
                  wo_ref, x_hbm, pg_ref, ng_ref, o_hbm, h_hbm, y_buf, acc_ref, xbuf, obuf,
                  hbuf, sems, *, e, idx):
    tm = a_ref.shape[0]

    def make_y(y_ref):
        mu = jnp.sum(s1_ref[...], axis=-1, keepdims=True) * (1.0 / e)
        var = jnp.sum(s2_ref[...], axis=-1, keepdims=True) * (1.0 / e) - mu * mu
        rstd = lax.rsqrt(jnp.maximum(var, 0.0) + EPS)
        g = g_ref[...].astype(F32)
        vn = ((g - mu) * rstd * lng_ref[...] + lnb_ref[...]).astype(BF16)
        heads_per_block = ws_ref.shape[0]
        hd = g_ref.shape[1] // heads_per_block
        t_idx = lax.broadcasted_iota(jnp.int32, (CHUNK, CHUNK), 0)
        s_idx = lax.broadcasted_iota(jnp.int32, (CHUNK, CHUNK), 1)
        for hh in range(heads_per_block):
            cols = slice(hh * hd, (hh + 1) * hd)
            wc = jnp.where(t_idx >= s_idx, ws_ref[hh], 0.0).astype(BF16)
            bs = bs_ref[hh]
            for c in range(tm // CHUNK):
                rows = slice(c * CHUNK, (c + 1) * CHUNK)
                sv = _dot(wc, vn[rows, cols]) + bs
                y_ref[rows, cols] = (a_ref[rows, cols].astype(F32) * sv).astype(BF16)

    _out_projection_pipeline(idx, make_y, y_buf, wo_ref, x_hbm, pg_ref, o_hbm, acc_ref,
                             xbuf, obuf, sems, next_gain_ref=ng_ref, h_hbm=h_hbm, hbuf=hbuf)


def _a_out(a, g, s1, s2, ln_g, ln_b, w_s, b_s, w_out, x2d, post_gain, next_gain, tm=512,
           heads_per_block=2):
    m, e = a.shape
    d = w_out.shape[1]
    heads = w_s.shape[0]
    hd = (e // heads) * heads_per_block
    idx = _PipelineIndex(m // tm, heads // heads_per_block)
    made_rc = lambda t: idx.made(t)
    made_row = lambda t: (idx.made(t)[0], 0)
    made_col = lambda t: (0, idx.made(t)[1])
    made_head = lambda t: (idx.made(t)[1], 0, 0)
    const = lambda t: (0, 0)
    in_hbm = pl.BlockSpec(memory_space=pl.ANY)
    return pl.pallas_call(
        functools.partial(_a_out_kernel, e=e, idx=idx),
        grid=(idx.total + 1,),
        in_specs=[pl.BlockSpec((tm, hd), made_rc),
                  pl.BlockSpec((tm, hd), made_rc),
                  pl.BlockSpec((tm, LANES), made_row),
                  pl.BlockSpec((tm, LANES), made_row),
                  pl.BlockSpec((1, hd), made_col),
                  pl.BlockSpec((1, hd), made_col),
                  pl.BlockSpec((heads_per_block, CHUNK, CHUNK), made_head),
                  pl.BlockSpec((heads_per_block, CHUNK, 1), made_head),
                  pl.BlockSpec((hd, d), lambda t: (idx.projected(t)[1], 0)),
                  in_hbm,
                  pl.BlockSpec((1, d), const),
                  pl.BlockSpec((1, d), const)],
        out_specs=[in_hbm, in_hbm],
        out_shape=[jax.ShapeDtypeStruct((m, d), F32),
                   jax.ShapeDtypeStruct((m // 2, d), jnp.uint32)],
        scratch_shapes=[pltpu.VMEM((2, tm, hd), BF16),
                        pltpu.VMEM((tm, d), F32),
                        pltpu.VMEM((tm, d), F32),
                        pltpu.VMEM((2, EPILOGUE_ROWS, d), F32),
                        pltpu.VMEM((2, EPILOGUE_ROWS // 2, d), jnp.uint32),
                        pltpu.SemaphoreType.DMA((3, 2))],
        compiler_params=_params(("arbitrary",), 56),
        name="a_out",
    )(a, g, s1, s2, ln_g.reshape(1, e), ln_b.reshape(1, e), w_s,
      b_s.reshape(heads, CHUNK, 1), w_out, x2d, post_gain.reshape(1, d),
      next_gain.reshape(1, d))


def _trailing_window_sums(ext, window):
    s = ext
    span = 1
    while span < window:
        s = s + pltpu.roll(s, span, 0)
        span *= 2
    assert span == window, "pool windows must be powers of two"
    return s


def _b_in_kernel(h_ref, wp_ref, wz_ref, wgf_ref, wof_ref, pooled_ref, sz_ref, wgb_ref,
                 wob_ref, carry_ref, *, seq, group_dim):
    j = pl.program_id(0)
    i = pl.program_id(1)
    tm, tn = pooled_ref.shape
    wgb_ref[...] = wgf_ref[...].astype(BF16)
    wob_ref[...] = wof_ref[...].astype(BF16)

    @pl.when(i == 0)
    def _():
        carry_ref[...] = jnp.zeros(carry_ref.shape, F32)

    tile_in_seq = i % (seq // tm)
    rc = tm // ROW_CHUNKS
    head = slice(0, MAX_WINDOW)
    pos_head = tile_in_seq * tm + 1 + lax.broadcasted_iota(jnp.int32, (MAX_WINDOW, 1), 0)

    def body(window):
        halo = jnp.where(tile_in_seq == 0, 0.0, carry_ref[...])
        for r in range(ROW_CHUNKS):
            rows = slice(r * rc, (r + 1) * rc)
            h = pltpu.bitcast(h_ref[r * rc // 2:(r + 1) * rc // 2, :], BF16)
            p = _dot(h, wp_ref[...])
            z = _dot(h, wz_ref[...])
            sz_ref[rows, :] = _silu(z).astype(sz_ref.dtype)
            ext = jnp.concatenate([halo, p], axis=0)
            sums = _trailing_window_sums(ext, window)[MAX_WINDOW:, :]
            pooled_ref[rows, :] = (sums * (1.0 / window) - p).astype(pooled_ref.dtype)
            if r == 0:
                inv_cnt = 1.0 / jnp.minimum(pos_head, window).astype(F32)
                pooled_ref[head, :] = (sums[head, :] * inv_cnt
                                       - p[head, :]).astype(pooled_ref.dtype)
            halo = p[rc - MAX_WINDOW:, :]
        carry_ref[...] = halo

    grp = j // (group_dim // tn)
    for gi, window in enumerate(POOL_WINDOWS):
        pl.when(grp == gi)(functools.partial(body, window))


def _b_in(h, w_in, w_grp, w_out, e, seq, tm=1024, tn=512):
    m, d = 2 * h.shape[0], h.shape[1]
    nj = e // tn
    ni = m // tm
    group_dim = e // len(POOL_WINDOWS)
    wg2d = w_grp.reshape(-1, w_grp.shape[-1])
    slab_g = wg2d.shape[0] // (nj * ni)
    slab_o = w_out.shape[0] // (nj * ni)
    step = lambda j, i: (j * ni + i, 0)
    return pl.pallas_call(
        functools.partial(_b_in_kernel, seq=seq, group_dim=group_dim),
        grid=(nj, ni),
        in_specs=[pl.BlockSpec((tm // 2, d), lambda j, i: (i, 0)),
                  pl.BlockSpec((d, tn), lambda j, i: (0, j)),
                  pl.BlockSpec((d, tn), lambda j, i: (0, nj + j)),
                  pl.BlockSpec((slab_g, wg2d.shape[1]), step),
                  pl.BlockSpec((slab_o, w_out.shape[1]), step)],
        out_specs=[pl.BlockSpec((tm, tn), lambda j, i: (i, j)),
                   pl.BlockSpec((tm, tn), lambda j, i: (i, j)),
                   pl.BlockSpec((slab_g, wg2d.shape[1]), step),
                   pl.BlockSpec((slab_o, w_out.shape[1]), step)],
        out_shape=[jax.ShapeDtypeStruct((m, e), BF16),
                   jax.ShapeDtypeStruct((m, e), BF16),
                   jax.ShapeDtypeStruct(wg2d.shape, BF16),
                   jax.ShapeDtypeStruct(w_out.shape, BF16)],
        scratch_shapes=[pltpu.VMEM((MAX_WINDOW, tn), F32)],
        compiler_params=_params(("arbitrary", "arbitrary"), 58),
        name="b_in",
    )(h, w_in, w_in, wg2d, w_out)


def _b_out_kernel(pooled_ref, wg_ref, bg_ref, sc_ref, sz_ref, wo_ref, x_hbm, pg_ref,
                  o_hbm, y_buf, acc_ref, xbuf, obuf, sems, *, idx):
    def make_y(y_ref):
        mixed = _dot(pooled_ref[...], wg_ref[0]) + bg_ref[...]
        y_ref[...] = (mixed * sc_ref[...] * sz_ref[...].astype(F32)).astype(BF16)

    _out_projection_pipeline(idx, make_y, y_buf, wo_ref, x_hbm, pg_ref, o_hbm, acc_ref,
                             xbuf, obuf, sems)


def _b_out(pooled, w_grp, b_grp, scale, sz, w_out, x2d, post_gain, tm=512, tn=1024):
    m, e = pooled.shape
    d = w_out.shape[1]
    groups, gw, _ = w_grp.shape
    nn = gw // tn
    idx = _PipelineIndex(m // tm, groups * nn)
    made_rc = lambda t: idx.made(t)
    made_col = lambda t: (0, idx.made(t)[1])
    in_hbm = pl.BlockSpec(memory_space=pl.ANY)
    return pl.pallas_call(
        functools.partial(_b_out_kernel, idx=idx),
        grid=(idx.total + 1,),
        in_specs=[pl.BlockSpec((tm, gw), lambda t: (idx.made(t)[0], idx.made(t)[1] // nn)),
                  pl.BlockSpec((1, gw, tn),
                               lambda t: (idx.made(t)[1] // nn, 0, idx.made(t)[1] % nn)),
                  pl.BlockSpec((1, tn), made_col),
                  pl.BlockSpec((1, tn), made_col),
                  pl.BlockSpec((tm, tn), made_rc),
                  pl.BlockSpec((tn, d), lambda t: (idx.projected(t)[1], 0)),
                  in_hbm,
                  pl.BlockSpec((1, d), lambda t: (0, 0))],
        out_specs=in_hbm,
        out_shape=jax.ShapeDtypeStruct((m, d), F32),
        scratch_shapes=[pltpu.VMEM((2, tm, tn), BF16),
                        pltpu.VMEM((tm, d), F32),
                        pltpu.VMEM((tm, d), F32),
                        pltpu.VMEM((2, EPILOGUE_ROWS, d), F32),
                        pltpu.SemaphoreType.DMA((3, 2))],
        compiler_params=_params(("arbitrary",), 56),
        name="b_out",
    )(pooled, w_grp, b_grp.reshape(1, e), scale.reshape(1, e), sz, w_out, x2d,
      post_gain.reshape(1, d))


def kernel(x, pre_norm, post_norm, a_w_in, a_ln_g, a_ln_b, a_w_s, a_b_s, a_w_out,
           b_w_in, b_w_grp, b_b_grp, b_scale, b_w_out):
    bsz, seq, d = x.shape
    e = a_w_out.shape[1]
    x2d = x.reshape(bsz * seq, d)

    h = _prenorm(x2d, pre_norm[0])
    a, g, s1, s2, a_wo, b_wi = _a_in(h, a_w_in[0], a_w_out[0], b_w_in[0], e)
    x2d, h = _a_out(a, g, s1, s2, a_ln_g[0], a_ln_b[0], a_w_s[0], a_b_s[0],
                    a_wo, x2d, post_norm[0], pre_norm[1])

    pooled, sz, b_wg, b_wo = _b_in(h, b_wi, b_w_grp[0], b_w_out[0], e, seq)
    x2d = _b_out(pooled, b_wg.reshape(b_w_grp[0].shape), b_b_grp[0], b_scale[0], sz,
                 b_wo, x2d, post_norm[1])
    return x2d.reshape(bsz, seq, d)
```

```python
import functools
import math

import jax
import jax.numpy as jnp
from jax import lax
from jax.experimental import pallas as pl
from jax.experimental.pallas import tpu as pltpu

EPS = 1e-6
CHUNK = 128
POOL_WINDOWS = (2, 4, 8, 16)
MAX_WINDOW = max(POOL_WINDOWS)
LANES = 128
MIB = 1024 * 1024

F32 = jnp.float32
BF16 = jnp.bfloat16


def _params(semantics, vmem_mib):
    return pltpu.CompilerParams(dimension_semantics=semantics,
                                vmem_limit_bytes=vmem_mib * MIB)


def _gelu(x):
    return 0.5 * x * (1.0 + lax.erf(x * math.sqrt(0.5)))


def _silu(x):
    return x / (1.0 + jnp.exp(-x))


def _dot(a, b):
    return jnp.dot(a, b, preferred_element_type=F32)


ROW_CHUNKS = 4


def _a_in_kernel(x_hbm, pre_ref, wu_ref, wv_ref, wz_ref, wof_ref, wnf_ref, a_ref, g_ref,
                 s1_ref, s2_ref, wob_ref, wnb_ref, h_ref, xbuf, sems):
    i = pl.program_id(0)
    j = pl.program_id(1)
    ni = pl.num_programs(0)
    nj = pl.num_programs(1)
    tm, tn = a_ref.shape
    rc = tm // ROW_CHUNKS
    assert ROW_CHUNKS >= 2 and ROW_CHUNKS % 2 == 0

    def x_copy(row_tile, r):
        rows = pl.ds(pl.multiple_of((row_tile * ROW_CHUNKS + r) * rc, rc), rc)
        return pltpu.make_async_copy(x_hbm.at[rows, :], xbuf.at[r % 2], sems.at[r % 2])

    @pl.when(jnp.logical_and(i == 0, j == 0))
    def _():
        x_copy(0, 0).start()
        x_copy(0, 1).start()

    @pl.when(jnp.logical_and(j == nj - 1, i + 1 < ni))
    def _():
        x_copy(i + 1, 0).start()
        x_copy(i + 1, 1).start()

    def body(first_step):
        wob_ref[...] = wof_ref[...].astype(BF16)
        wnb_ref[...] = wnf_ref[...].astype(BF16)
        wu = wu_ref[...].astype(BF16)
        wv = wv_ref[...].astype(BF16)
        wz = wz_ref[...].astype(BF16)
        p1s, p2s = [], []
        for r in range(ROW_CHUNKS):
            rows = slice(r * rc, (r + 1) * rc)
            packed_rows = slice(r * rc // 2, (r + 1) * rc // 2)
            if first_step:
                x_copy(i, r).wait()
                x = xbuf[r % 2]
                ms = jnp.mean(x * x, axis=-1, keepdims=True)
                h = (x * lax.rsqrt(ms + EPS) * pre_ref[...]).astype(BF16)
                h_ref[packed_rows, :] = pltpu.bitcast(h, jnp.uint32)
                if r + 2 < ROW_CHUNKS:
                    x_copy(i, r + 2).start()
            else:
                h = pltpu.bitcast(h_ref[packed_rows, :], BF16)
            u = _dot(h, wu)
            v = _dot(h, wv)
            z = _dot(h, wz)
            a_ref[rows, :] = (_gelu(u) * _silu(z)).astype(a_ref.dtype)
            gv = _gelu(v)
            g_ref[rows, :] = gv.astype(g_ref.dtype)
            q1 = gv[:, 0:LANES]
            q2 = q1 * q1
            for k in range(1, tn // LANES):
                blk = gv[:, k * LANES:(k + 1) * LANES]
                q1 = q1 + blk
                q2 = q2 + blk * blk
            p1s.append(q1)
            p2s.append(q2)
        p1 = jnp.concatenate(p1s, axis=0)
        p2 = jnp.concatenate(p2s, axis=0)
        if first_step:
            s1_ref[...] = p1
            s2_ref[...] = p2
        else:
            s1_ref[...] += p1
            s2_ref[...] += p2

    pl.when(j == 0)(functools.partial(body, True))
    pl.when(j > 0)(functools.partial(body, False))


def _a_in(x2d, pre_gain, w_in, w_out, w_next, e, tm=1024, tn=256):
    m, d = x2d.shape
    nj = e // tn
    ni = m // tm
    slab_o = w_out.shape[0] // (ni * nj)
    slab_n = w_next.shape[0] // (ni * nj)
    row = lambda i, j: (i, 0)
    step = lambda i, j: (i * nj + j, 0)
    return pl.pallas_call(
        _a_in_kernel,
        grid=(ni, nj),
        in_specs=[pl.BlockSpec(memory_space=pl.ANY),
                  pl.BlockSpec((1, d), lambda i, j: (0, 0)),
                  pl.BlockSpec((d, tn), lambda i, j: (0, j)),
                  pl.BlockSpec((d, tn), lambda i, j: (0, nj + j)),
                  pl.BlockSpec((d, tn), lambda i, j: (0, 2 * nj + j)),
                  pl.BlockSpec((slab_o, w_out.shape[1]), step),
                  pl.BlockSpec((slab_n, w_next.shape[1]), step)],
        out_specs=[pl.BlockSpec((tm, tn), lambda i, j: (i, j)),
                   pl.BlockSpec((tm, tn), lambda i, j: (i, j)),
                   pl.BlockSpec((tm, LANES), row),
                   pl.BlockSpec((tm, LANES), row),
                   pl.BlockSpec((slab_o, w_out.shape[1]), step),
                   pl.BlockSpec((slab_n, w_next.shape[1]), step)],
        out_shape=[jax.ShapeDtypeStruct((m, e), BF16),
                   jax.ShapeDtypeStruct((m, e), BF16),
                   jax.ShapeDtypeStruct((m, LANES), F32),
                   jax.ShapeDtypeStruct((m, LANES), F32),
                   jax.ShapeDtypeStruct(w_out.shape, BF16),
                   jax.ShapeDtypeStruct(w_next.shape, BF16)],
        scratch_shapes=[pltpu.VMEM((tm // 2, d), jnp.uint32),
                        pltpu.VMEM((2, tm // ROW_CHUNKS, d), F32),
                        pltpu.SemaphoreType.DMA((2,))],
        compiler_params=_params(("arbitrary", "arbitrary"), 58),
        name="a_in",
    )(x2d, pre_gain.reshape(1, d), w_in, w_in, w_in, w_out, w_next)


OUT_COL_SPLIT = 4
EPILOGUE_ROWS = 128


class _PipelineIndex:
    def __init__(self, n_rows, n_blocks):
        self.n_blocks = n_blocks
        self.total = n_rows * n_blocks

    def made(self, t):
        tt = jnp.minimum(t, self.total - 1)
        return tt // self.n_blocks, tt % self.n_blocks

    def projected(self, t):
        tt = jnp.maximum(t - 1, 0)
        return tt // self.n_blocks, tt % self.n_blocks


X_SEM, O_SEM, H_SEM = 0, 1, 2


def _out_projection_pipeline(idx, make_y, y_buf, wo_ref, x_hbm, pg_ref, o_hbm, acc_ref,
                             xbuf, obuf, sems, next_gain_ref=None, h_hbm=None, hbuf=None):
    t = pl.program_id(0)
    tm, d = acc_ref.shape
    dn = d // OUT_COL_SPLIT
    n_chunks = tm // EPILOGUE_ROWS
    half = EPILOGUE_ROWS // 2
    slot = t % 2
    row_p, blk_p = idx.projected(t)
    row_tile_done = jnp.logical_and(t > 0, blk_p == idx.n_blocks - 1)

    def chunk_rows(c, packed=False):
        size = half if packed else EPILOGUE_ROWS
        return pl.ds(pl.multiple_of((row_p * n_chunks + c) * size, size), size)

    def x_copy():
        rows = pl.ds(pl.multiple_of(row_p * tm, tm), tm)
        return pltpu.make_async_copy(x_hbm.at[rows, :], xbuf, sems.at[X_SEM, 0])

    def o_copy(c):
        return pltpu.make_async_copy(obuf.at[c % 2], o_hbm.at[chunk_rows(c), :],
                                     sems.at[O_SEM, c % 2])

    def h_copy(c):
        return pltpu.make_async_copy(hbuf.at[c % 2], h_hbm.at[chunk_rows(c, packed=True), :],
                                     sems.at[H_SEM, c % 2])

    def wait_sent(c):
        o_copy(c).wait()
        if h_hbm is not None:
            h_copy(c).wait()

    def project():
        y = y_buf[1 - slot]
        for nb in range(OUT_COL_SPLIT):
            cols = slice(nb * dn, (nb + 1) * dn)
            acc_ref[:, cols] += _dot(y, wo_ref[:, cols])

    @pl.when(jnp.logical_and(t > 0, blk_p == 0))
    def _():
        x_copy().start()
        acc_ref[...] = jnp.zeros(acc_ref.shape, acc_ref.dtype)

    @pl.when(t == 0)
    def _():
        make_y(y_buf.at[slot])

    @pl.when(jnp.logical_and(t > 0, t < idx.total))
    def _():
        project()
        make_y(y_buf.at[slot])

    @pl.when(t == idx.total)
    def _():
        project()

    assert n_chunks >= 2 and n_chunks % 2 == 0

    @pl.when(jnp.logical_and(row_tile_done, row_p > 0))
    def _():
        wait_sent(0)
        wait_sent(1)

    @pl.when(row_tile_done)
    def _():
        x_copy().wait()
        for c in range(n_chunks):
            if c >= 2:
                wait_sent(c - 2)
            rows = slice(c * EPILOGUE_ROWS, (c + 1) * EPILOGUE_ROWS)
            mm = acc_ref[rows, :]
            ms = jnp.mean(mm * mm, axis=-1, keepdims=True)
            xn = xbuf[rows, :] + mm * lax.rsqrt(ms + EPS) * pg_ref[...]
            obuf[c % 2] = xn
            o_copy(c).start()
            if h_hbm is not None:
                ms_next = jnp.mean(xn * xn, axis=-1, keepdims=True)
                hn = (xn * lax.rsqrt(ms_next + EPS) * next_gain_ref[...]).astype(BF16)
                hbuf[c % 2] = pltpu.bitcast(hn, jnp.uint32)
                h_copy(c).start()

    @pl.when(t == idx.total)
    def _():
        wait_sent(n_chunks - 2)
        wait_sent(n_chunks - 1)


def _a_out_kernel(a_ref, g_ref, s1_ref, s2_ref, lng_ref, lnb_ref, ws_ref, bs_ref,
                  wo_ref, x_hbm, pg_ref, ng_ref, o_hbm, h_hbm, y_buf, acc_ref, xbuf, obuf,
                  hbuf, sems, *, e, idx):
    tm = a_ref.shape[0]

    def make_y(y_ref):
        mu = jnp.sum(s1_ref[...], axis=-1, keepdims=True) * (1.0 / e)
        var = jnp.sum(s2_ref[...], axis=-1, keepdims=True) * (1.0 / e) - mu * mu
        rstd = lax.rsqrt(jnp.maximum(var, 0.0) + EPS)
        g = g_ref[...].astype(F32)
        vn = ((g - mu) * rstd * lng_ref[...] + lnb_ref[...]).astype(BF16)
        heads_per_block = ws_ref.shape[0]
        hd = g_ref.shape[1] // heads_per_block
        t_idx = lax.broadcasted_iota(jnp.int32, (CHUNK, CHUNK), 0)
        s_idx = lax.broadcasted_iota(jnp.int32, (CHUNK, CHUNK), 1)
        for hh in range(heads_per_block):
            cols = slice(hh * hd, (hh + 1) * hd)
            wc = jnp.where(t_idx >= s_idx, ws_ref[hh], 0.0).astype(BF16)
            bs = bs_ref[hh]
            for c in range(tm // CHUNK):
                rows = slice(c * CHUNK, (c + 1) * CHUNK)
                sv = _dot(wc, vn[rows, cols]) + bs
                y_ref[rows, cols] = (a_ref[rows, cols].astype(F32) * sv).astype(BF16)

    _out_projection_pipeline(idx, make_y, y_buf, wo_ref, x_hbm, pg_ref, o_hbm, acc_ref,
                             xbuf, obuf, sems, next_gain_ref=ng_ref, h_hbm=h_hbm, hbuf=hbuf)


def _a_out(a, g, s1, s2, ln_g, ln_b, w_s, b_s, w_out, x2d, post_gain, next_gain, tm=512,
           heads_per_block=2):
    m, e = a.shape
    d = w_out.shape[1]
    heads = w_s.shape[0]
    hd = (e // heads) * heads_per_block
    idx = _PipelineIndex(m // tm, heads // heads_per_block)
    made_rc = lambda t: idx.made(t)
    made_row = lambda t: (idx.made(t)[0], 0)
    made_col = lambda t: (0, idx.made(t)[1])
    made_head = lambda t: (idx.made(t)[1], 0, 0)
    const = lambda t: (0, 0)
    in_hbm = pl.BlockSpec(memory_space=pl.ANY)
    return pl.pallas_call(
        functools.partial(_a_out_kernel, e=e, idx=idx),
        grid=(idx.total + 1,),
        in_specs=[pl.BlockSpec((tm, hd), made_rc),
                  pl.BlockSpec((tm, hd), made_rc),
                  pl.BlockSpec((tm, LANES), made_row),
                  pl.BlockSpec((tm, LANES), made_row),
                  pl.BlockSpec((1, hd), made_col),
                  pl.BlockSpec((1, hd), made_col),
                  pl.BlockSpec((heads_per_block, CHUNK, CHUNK), made_head),
                  pl.BlockSpec((heads_per_block, CHUNK, 1), made_head),
                  pl.BlockSpec((hd, d), lambda t: (idx.projected(t)[1], 0)),
                  in_hbm,
                  pl.BlockSpec((1, d), const),
                  pl.BlockSpec((1, d), const)],
        out_specs=[in_hbm, in_hbm],
        out_shape=[jax.ShapeDtypeStruct((m, d), F32),
                   jax.ShapeDtypeStruct((m // 2, d), jnp.uint32)],
        scratch_shapes=[pltpu.VMEM((2, tm, hd), BF16),
                        pltpu.VMEM((tm, d), F32),
                        pltpu.VMEM((tm, d), F32),
                        pltpu.VMEM((2, EPILOGUE_ROWS, d), F32),
                        pltpu.VMEM((2, EPILOGUE_ROWS // 2, d), jnp.uint32),
                        pltpu.SemaphoreType.DMA((3, 2))],
        compiler_params=_params(("arbitrary",), 56),
        name="a_out",
    )(a, g, s1, s2, ln_g.reshape(1, e), ln_b.reshape(1, e), w_s,
      b_s.reshape(heads, CHUNK, 1), w_out, x2d, post_gain.reshape(1, d),
      next_gain.reshape(1, d))


def _trailing_window_sums(ext, window):
    s = ext
    span = 1
    while span < window:
        s = s + pltpu.roll(s, span, 0)
        span *= 2
    assert span == window, "pool windows must be powers of two"
    return s


def _b_in_kernel(h_ref, wp_ref, wz_ref, wgf_ref, wof_ref, pooled_ref, sz_ref, wgb_ref,
                 wob_ref, carry_ref, *, seq, group_dim):
    j = pl.program_id(0)
    i = pl.program_id(1)
    tm, tn = pooled_ref.shape
    wgb_ref[...] = wgf_ref[...].astype(BF16)
    wob_ref[...] = wof_ref[...].astype(BF16)

    @pl.when(i == 0)
    def _():
        carry_ref[...] = jnp.zeros(carry_ref.shape, F32)

    tile_in_seq = i % (seq // tm)
    rc = tm // ROW_CHUNKS
    head = slice(0, MAX_WINDOW)
    pos_head = tile_in_seq * tm + 1 + lax.broadcasted_iota(jnp.int32, (MAX_WINDOW, 1), 0)

    def body(window):
        halo = jnp.where(tile_in_seq == 0, 0.0, carry_ref[...])
        for r in range(ROW_CHUNKS):
            rows = slice(r * rc, (r + 1) * rc)
            h = pltpu.bitcast(h_ref[r * rc // 2:(r + 1) * rc // 2, :], BF16)
            p = _dot(h, wp_ref[...])
            z = _dot(h, wz_ref[...])
            sz_ref[rows, :] = _silu(z).astype(sz_ref.dtype)
            ext = jnp.concatenate([halo, p], axis=0)
            sums = _trailing_window_sums(ext, window)[MAX_WINDOW:, :]
            pooled_ref[rows, :] = (sums * (1.0 / window) - p).astype(pooled_ref.dtype)
            if r == 0:
                inv_cnt = 1.0 / jnp.minimum(pos_head, window).astype(F32)
                pooled_ref[head, :] = (sums[head, :] * inv_cnt
                                       - p[head, :]).astype(pooled_ref.dtype)
            halo = p[rc - MAX_WINDOW:, :]
        carry_ref[...] = halo

    grp = j // (group_dim // tn)
    for gi, window in enumerate(POOL_WINDOWS):
        pl.when(grp == gi)(functools.partial(body, window))


def _b_in(h, w_in, w_grp, w_out, e, seq, tm=1024, tn=512):
    m, d = 2 * h.shape[0], h.shape[1]
    nj = e // tn
    ni = m // tm
    group_dim = e // len(POOL_WINDOWS)
    wg2d = w_grp.reshape(-1, w_grp.shape[-1])
    slab_g = wg2d.shape[0] // (nj * ni)
    slab_o = w_out.shape[0] // (nj * ni)
    step = lambda j, i: (j * ni + i, 0)
    return pl.pallas_call(
        functools.partial(_b_in_kernel, seq=seq, group_dim=group_dim),
        grid=(nj, ni),
        in_specs=[pl.BlockSpec((tm // 2, d), lambda j, i: (i, 0)),
                  pl.BlockSpec((d, tn), lambda j, i: (0, j)),
                  pl.BlockSpec((d, tn), lambda j, i: (0, nj + j)),
                  pl.BlockSpec((slab_g, wg2d.shape[1]), step),
                  pl.BlockSpec((slab_o, w_out.shape[1]), step)],
        out_specs=[pl.BlockSpec((tm, tn), lambda j, i: (i, j)),
                   pl.BlockSpec((tm, tn), lambda j, i: (i, j)),
                   pl.BlockSpec((slab_g, wg2d.shape[1]), step),
                   pl.BlockSpec((slab_o, w_out.shape[1]), step)],
        out_shape=[jax.ShapeDtypeStruct((m, e), BF16),
                   jax.ShapeDtypeStruct((m, e), BF16),
                   jax.ShapeDtypeStruct(wg2d.shape, BF16),
                   jax.ShapeDtypeStruct(w_out.shape, BF16)],
        scratch_shapes=[pltpu.VMEM((MAX_WINDOW, tn), F32)],
        compiler_params=_params(("arbitrary", "arbitrary"), 58),
        name="b_in",
    )(h, w_in, w_in, wg2d, w_out)


def _b_out_kernel(pooled_ref, wg_ref, bg_ref, sc_ref, sz_ref, wo_ref, x_hbm, pg_ref,
                  o_hbm, y_buf, acc_ref, xbuf, obuf, sems, *, idx):
    def make_y(y_ref):
        mixed = _dot(pooled_ref[...], wg_ref[0]) + bg_ref[...]
        y_ref[...] = (mixed * sc_ref[...] * sz_ref[...].astype(F32)).astype(BF16)

    _out_projection_pipeline(idx, make_y, y_buf, wo_ref, x_hbm, pg_ref, o_hbm, acc_ref,
                             xbuf, obuf, sems)


def _b_out(pooled, w_grp, b_grp, scale, sz, w_out, x2d, post_gain, tm=512, tn=1024):
    m, e = pooled.shape
    d = w_out.shape[1]
    groups, gw, _ = w_grp.shape
    nn = gw // tn
    idx = _PipelineIndex(m // tm, groups * nn)
    made_rc = lambda t: idx.made(t)
    made_col = lambda t: (0, idx.made(t)[1])
    in_hbm = pl.BlockSpec(memory_space=pl.ANY)
    return pl.pallas_call(
        functools.partial(_b_out_kernel, idx=idx),
        grid=(idx.total + 1,),
        in_specs=[pl.BlockSpec((tm, gw), lambda t: (idx.made(t)[0], idx.made(t)[1] // nn)),
                  pl.BlockSpec((1, gw, tn),
                               lambda t: (idx.made(t)[1] // nn, 0, idx.made(t)[1] % nn)),
                  pl.BlockSpec((1, tn), made_col),
                  pl.BlockSpec((1, tn), made_col),
                  pl.BlockSpec((tm, tn), made_rc),
                  pl.BlockSpec((tn, d), lambda t: (idx.projected(t)[1], 0)),
                  in_hbm,
                  pl.BlockSpec((1, d), lambda t: (0, 0))],
        out_specs=in_hbm,
        out_shape=jax.ShapeDtypeStruct((m, d), F32),
        scratch_shapes=[pltpu.VMEM((2, tm, tn), BF16),
                        pltpu.VMEM((tm, d), F32),
                        pltpu.VMEM((tm, d), F32),
                        pltpu.VMEM((2, EPILOGUE_ROWS, d), F32),
                        pltpu.SemaphoreType.DMA((3, 2))],
        compiler_params=_params(("arbitrary",), 56),
        name="b_out",
    )(pooled, w_grp, b_grp.reshape(1, e), scale.reshape(1, e), sz, w_out, x2d,
      post_gain.reshape(1, d))


def kernel(x, pre_norm, post_norm, a_w_in, a_ln_g, a_ln_b, a_w_s, a_b_s, a_w_out,
           b_w_in, b_w_grp, b_b_grp, b_scale, b_w_out):
    bsz, seq, d = x.shape
    e = a_w_out.shape[1]
    x2d = x.reshape(bsz * seq, d)

    a, g, s1, s2, a_wo, b_wi = _a_in(x2d, pre_norm[0], a_w_in[0], a_w_out[0], b_w_in[0], e)
    x2d, h = _a_out(a, g, s1, s2, a_ln_g[0], a_ln_b[0], a_w_s[0], a_b_s[0],
                    a_wo, x2d, post_norm[0], pre_norm[1])

    pooled, sz, b_wg, b_wo = _b_in(h, b_wi, b_w_grp[0], b_w_out[0], e, seq)
    x2d = _b_out(pooled, b_wg.reshape(b_w_grp[0].shape), b_b_grp[0], b_scale[0], sz,
                 b_wo, x2d, post_norm[1])
    return x2d.reshape(bsz, seq, d)
```

```python
import functools
import math

import jax
import jax.numpy as jnp
from jax import lax
from jax.experimental import pallas as pl
from jax.experimental.pallas import tpu as pltpu

EPS = 1e-6
CHUNK = 128
POOL_WINDOWS = (2, 4, 8, 16)
MAX_WINDOW = max(POOL_WINDOWS)
LANES = 128
MIB = 1024 * 1024

F32 = jnp.float32
BF16 = jnp.bfloat16


def _params(semantics, vmem_mib):
    return pltpu.CompilerParams(dimension_semantics=semantics,
                                vmem_limit_bytes=vmem_mib * MIB)


def _gelu(x):
    return 0.5 * x * (1.0 + lax.erf(x * math.sqrt(0.5)))


def _silu(x):
    return x / (1.0 + jnp.exp(-x))


def _dot(a, b):
    return jnp.dot(a, b, preferred_element_type=F32)


ROW_CHUNKS = 4


def _a_in_kernel(x_hbm, pre_ref, wu_ref, wv_ref, wz_ref, wof_ref, wnf_ref, a_ref, g_ref,
                 s1_ref, s2_ref, wob_ref, wnb_ref, h_ref, xbuf, sems):
    i = pl.program_id(0)
    j = pl.program_id(1)
    ni = pl.num_programs(0)
    nj = pl.num_programs(1)
    tm, tn = a_ref.shape
    rc = tm // ROW_CHUNKS
    assert ROW_CHUNKS >= 2 and ROW_CHUNKS % 2 == 0

    def x_copy(row_tile, r):
        rows = pl.ds(pl.multiple_of((row_tile * ROW_CHUNKS + r) * rc, rc), rc)
        return pltpu.make_async_copy(x_hbm.at[rows, :], xbuf.at[r % 2], sems.at[r % 2])

    @pl.when(jnp.logical_and(i == 0, j == 0))
    def _():
        x_copy(0, 0).start()
        x_copy(0, 1).start()

    @pl.when(jnp.logical_and(j == nj - 1, i + 1 < ni))
    def _():
        x_copy(i + 1, 0).start()
        x_copy(i + 1, 1).start()

    def body(first_step):
        wob_ref[...] = wof_ref[...].astype(BF16)
        wnb_ref[...] = wnf_ref[...].astype(BF16)
        wu = wu_ref[...].astype(BF16)
        wv = wv_ref[...].astype(BF16)
        wz = wz_ref[...].astype(BF16)
        p1s, p2s = [], []
        for r in range(ROW_CHUNKS):
            rows = slice(r * rc, (r + 1) * rc)
            packed_rows = slice(r * rc // 2, (r + 1) * rc // 2)
            if first_step:
                x_copy(i, r).wait()
                x = xbuf[r % 2]
                ms = jnp.mean(x * x, axis=-1, keepdims=True)
                h = (x * lax.rsqrt(ms + EPS) * pre_ref[...]).astype(BF16)
                h_ref[packed_rows, :] = pltpu.bitcast(h, jnp.uint32)
                if r + 2 < ROW_CHUNKS:
                    x_copy(i, r + 2).start()
            else:
                h = pltpu.bitcast(h_ref[packed_rows, :], BF16)
            u = _dot(h, wu)
            v = _dot(h, wv)
            z = _dot(h, wz)
            a_ref[rows, :] = (_gelu(u) * _silu(z)).astype(a_ref.dtype)
            gv = _gelu(v)
            g_ref[rows, :] = gv.astype(g_ref.dtype)
            q1 = gv[:, 0:LANES]
            q2 = q1 * q1
            for k in range(1, tn // LANES):
                blk = gv[:, k * LANES:(k + 1) * LANES]
                q1 = q1 + blk
                q2 = q2 + blk * blk
            p1s.append(q1)
            p2s.append(q2)
        p1 = jnp.concatenate(p1s, axis=0)
        p2 = jnp.concatenate(p2s, axis=0)
        if first_step:
            s1_ref[...] = p1
            s2_ref[...] = p2
        else:
            s1_ref[...] += p1
            s2_ref[...] += p2

    pl.when(j == 0)(functools.partial(body, True))
    pl.when(j > 0)(functools.partial(body, False))


def _a_in(x2d, pre_gain, w_in, w_out, w_next, e, tm=1024, tn=256):
    m, d = x2d.shape
    nj = e // tn
    ni = m // tm
    slab_o = w_out.shape[0] // (ni * nj)
    slab_n = w_next.shape[0] // (ni * nj)
    row = lambda i, j: (i, 0)
    step = lambda i, j: (i * nj + j, 0)
    return pl.pallas_call(
        _a_in_kernel,
        grid=(ni, nj),
        in_specs=[pl.BlockSpec(memory_space=pl.ANY),
                  pl.BlockSpec((1, d), lambda i, j: (0, 0)),
                  pl.BlockSpec((d, tn), lambda i, j: (0, j)),
                  pl.BlockSpec((d, tn), lambda i, j: (0, nj + j)),
                  pl.BlockSpec((d, tn), lambda i, j: (0, 2 * nj + j)),
                  pl.BlockSpec((slab_o, w_out.shape[1]), step),
                  pl.BlockSpec((slab_n, w_next.shape[1]), step)],
        out_specs=[pl.BlockSpec((tm, tn), lambda i, j: (i, j)),
                   pl.BlockSpec((tm, tn), lambda i, j: (i, j)),
                   pl.BlockSpec((tm, LANES), row),
                   pl.BlockSpec((tm, LANES), row),
                   pl.BlockSpec((slab_o, w_out.shape[1]), step),
                   pl.BlockSpec((slab_n, w_next.shape[1]), step)],
        out_shape=[jax.ShapeDtypeStruct((m, e), BF16),
                   jax.ShapeDtypeStruct((m, e), BF16),
                   jax.ShapeDtypeStruct((m, LANES), F32),
                   jax.ShapeDtypeStruct((m, LANES), F32),
                   jax.ShapeDtypeStruct(w_out.shape, BF16),
                   jax.ShapeDtypeStruct(w_next.shape, BF16)],
        scratch_shapes=[pltpu.VMEM((tm // 2, d), jnp.uint32),
                        pltpu.VMEM((2, tm // ROW_CHUNKS, d), F32),
                        pltpu.SemaphoreType.DMA((2,))],
        compiler_params=_params(("arbitrary", "arbitrary"), 58),
        name="a_in",
    )(x2d, pre_gain.reshape(1, d), w_in, w_in, w_in, w_out, w_next)


OUT_COL_SPLIT = 4
EPILOGUE_ROWS = 128


class _PipelineIndex:
    def __init__(self, n_rows, n_blocks):
        self.n_blocks = n_blocks
        self.total = n_rows * n_blocks

    def made(self, t):
        tt = jnp.minimum(t, self.total - 1)
        return tt // self.n_blocks, tt % self.n_blocks

    def projected(self, t):
        tt = jnp.maximum(t - 1, 0)
        return tt // self.n_blocks, tt % self.n_blocks


X_SEM, O_SEM, H_SEM = 0, 1, 2


def _out_projection_pipeline(idx, make_y, y_buf, wo_ref, x_hbm, pg_ref, o_hbm, acc_ref,
                             xbuf, obuf, sems, next_gain_ref=None, h_hbm=None, hbuf=None):
    t = pl.program_id(0)
    tm, d = acc_ref.shape
    dn = d // OUT_COL_SPLIT
    n_chunks = tm // EPILOGUE_ROWS
    half = EPILOGUE_ROWS // 2
    slot = t % 2
    row_p, blk_p = idx.projected(t)
    row_tile_done = jnp.logical_and(t > 0, blk_p == idx.n_blocks - 1)

    def chunk_rows(c, packed=False):
        size = half if packed else EPILOGUE_ROWS
        return pl.ds(pl.multiple_of((row_p * n_chunks + c) * size, size), size)

    def x_copy():
        rows = pl.ds(pl.multiple_of(row_p * tm, tm), tm)
        return pltpu.make_async_copy(x_hbm.at[rows, :], xbuf, sems.at[X_SEM, 0])

    def o_copy(c):
        return pltpu.make_async_copy(obuf.at[c % 2], o_hbm.at[chunk_rows(c), :],
                                     sems.at[O_SEM, c % 2])

    def h_copy(c):
        return pltpu.make_async_copy(hbuf.at[c % 2], h_hbm.at[chunk_rows(c, packed=True), :],
                                     sems.at[H_SEM, c % 2])

    def wait_sent(c):
        o_copy(c).wait()
        if h_hbm is not None:
            h_copy(c).wait()

    def project():
        y = y_buf[1 - slot]
        for nb in range(OUT_COL_SPLIT):
            cols = slice(nb * dn, (nb + 1) * dn)
            acc_ref[:, cols] += _dot(y, wo_ref[:, cols])

    @pl.when(jnp.logical_and(t > 0, blk_p == 0))
    def _():
        x_copy().start()
        acc_ref[...] = jnp.zeros(acc_ref.shape, acc_ref.dtype)

    @pl.when(t == 0)
    def _():
        make_y(y_buf.at[slot])

    @pl.when(jnp.logical_and(t > 0, t < idx.total))
    def _():
        project()
        make_y(y_buf.at[slot])

    @pl.when(t == idx.total)
    def _():
        project()

    assert n_chunks >= 2 and n_chunks % 2 == 0

    @pl.when(jnp.logical_and(row_tile_done, row_p > 0))
    def _():
        wait_sent(0)
        wait_sent(1)

    @pl.when(row_tile_done)
    def _():
        x_copy().wait()
        for c in range(n_chunks):
            if c >= 2:
                wait_sent(c - 2)
            rows = slice(c * EPILOGUE_ROWS, (c + 1) * EPILOGUE_ROWS)
            mm = acc_ref[rows, :]
            ms = jnp.mean(mm * mm, axis=-1, keepdims=True)
            xn = xbuf[rows, :] + mm * lax.rsqrt(ms + EPS) * pg_ref[...]
            obuf[c % 2] = xn
            o_copy(c).start()
            if h_hbm is not None:
                ms_next = jnp.mean(xn * xn, axis=-1, keepdims=True)
                hn = (xn * lax.rsqrt(ms_next + EPS) * next_gain_ref[...]).astype(BF16)
                hbuf[c % 2] = pltpu.bitcast(hn, jnp.uint32)
                h_copy(c).start()

    @pl.when(t == idx.total)
    def _():
        wait_sent(n_chunks - 2)
        wait_sent(n_chunks - 1)


def _a_out_kernel(a_ref, g_ref, s1_ref, s2_ref, lng_ref, lnb_ref, ws_ref, bs_ref,
                  wo_ref, x_hbm, pg_ref, ng_ref, o_hbm, h_hbm, y_buf, acc_ref, xbuf, obuf,
                  hbuf, sems, *, e, idx):
    tm = a_ref.shape[0]

    def make_y(y_ref):
        mu = jnp.sum(s1_ref[...], axis=-1, keepdims=True) * (1.0 / e)
        var = jnp.sum(s2_ref[...], axis=-1, keepdims=True) * (1.0 / e) - mu * mu
        rstd = lax.rsqrt(jnp.maximum(var, 0.0) + EPS)
        g = g_ref[...].astype(F32)
        vn = ((g - mu) * rstd * lng_ref[...] + lnb_ref[...]).astype(BF16)
        heads_per_block = ws_ref.shape[0]
        hd = g_ref.shape[1] // heads_per_block
        t_idx = lax.broadcasted_iota(jnp.int32, (CHUNK, CHUNK), 0)
        s_idx = lax.broadcasted_iota(jnp.int32, (CHUNK, CHUNK), 1)
        for hh in range(heads_per_block):
            cols = slice(hh * hd, (hh + 1) * hd)
            wc = jnp.where(t_idx >= s_idx, ws_ref[hh], 0.0).astype(BF16)
            bs = bs_ref[hh]
            for c in range(tm // CHUNK):
                rows = slice(c * CHUNK, (c + 1) * CHUNK)
                sv = _dot(wc, vn[rows, cols]) + bs
                y_ref[rows, cols] = (a_ref[rows, cols].astype(F32) * sv).astype(BF16)

    _out_projection_pipeline(idx, make_y, y_buf, wo_ref, x_hbm, pg_ref, o_hbm, acc_ref,
                             xbuf, obuf, sems, next_gain_ref=ng_ref, h_hbm=h_hbm, hbuf=hbuf)


def _a_out(a, g, s1, s2, ln_g, ln_b, w_s, b_s, w_out, x2d, post_gain, next_gain, tm=512,
           heads_per_block=2):
    m, e = a.shape
    d = w_out.shape[1]
    heads = w_s.shape[0]
    hd = (e // heads) * heads_per_block
    idx = _PipelineIndex(m // tm, heads // heads_per_block)
    made_rc = lambda t: idx.made(t)
    made_row = lambda t: (idx.made(t)[0], 0)
    made_col = lambda t: (0, idx.made(t)[1])
    made_head = lambda t: (idx.made(t)[1], 0, 0)
    const = lambda t: (0, 0)
    in_hbm = pl.BlockSpec(memory_space=pl.ANY)
    return pl.pallas_call(
        functools.partial(_a_out_kernel, e=e, idx=idx),
        grid=(idx.total + 1,),
        in_specs=[pl.BlockSpec((tm, hd), made_rc),
                  pl.BlockSpec((tm, hd), made_rc),
                  pl.BlockSpec((tm, LANES), made_row),
                  pl.BlockSpec((tm, LANES), made_row),
                  pl.BlockSpec((1, hd), made_col),
                  pl.BlockSpec((1, hd), made_col),
                  pl.BlockSpec((heads_per_block, CHUNK, CHUNK), made_head),
                  pl.BlockSpec((heads_per_block, CHUNK, 1), made_head),
                  pl.BlockSpec((hd, d), lambda t: (idx.projected(t)[1], 0)),
                  in_hbm,
                  pl.BlockSpec((1, d), const),
                  pl.BlockSpec((1, d), const)],
        out_specs=[in_hbm, in_hbm],
        out_shape=[jax.ShapeDtypeStruct((m, d), F32),
                   jax.ShapeDtypeStruct((m // 2, d), jnp.uint32)],
        scratch_shapes=[pltpu.VMEM((2, tm, hd), BF16),
                        pltpu.VMEM((tm, d), F32),
                        pltpu.VMEM((tm, d), F32),
                        pltpu.VMEM((2, EPILOGUE_ROWS, d), F32),
                        pltpu.VMEM((2, EPILOGUE_ROWS // 2, d), jnp.uint32),
                        pltpu.SemaphoreType.DMA((3, 2))],
        compiler_params=_params(("arbitrary",), 56),
        name="a_out",
    )(a, g, s1, s2, ln_g.reshape(1, e), ln_b.reshape(1, e), w_s,
      b_s.reshape(heads, CHUNK, 1), w_out, x2d, post_gain.reshape(1, d),
      next_gain.reshape(1, d))


def _trailing_window_sums(ext, window):
    s = ext
    span = 1
    while span < window:
        s = s + pltpu.roll(s, span, 0)
        span *= 2
    assert span == window, "pool windows must be powers of two"
    return s


def _b_in_step(step_idx, h_ref, wp_ref, wz_ref, wgf_ref, wof_ref, pooled_ref, sz_ref,
               wgb_ref, wob_ref, *, carry_ref, seq, group_dim):
    j, i = step_idx
    tm, tn = pooled_ref.shape
    wgb_ref[...] = wgf_ref[...].astype(BF16)
    wob_ref[...] = wof_ref[...].astype(BF16)

    @pl.when(i == 0)
    def _():
        carry_ref[...] = jnp.zeros(carry_ref.shape, F32)

    tile_in_seq = i % (seq // tm)
    rc = tm // ROW_CHUNKS
    head = slice(0, MAX_WINDOW)
    pos_head = tile_in_seq * tm + 1 + lax.broadcasted_iota(jnp.int32, (MAX_WINDOW, 1), 0)

    def body(window):
        halo = jnp.where(tile_in_seq == 0, 0.0, carry_ref[...])
        for r in range(ROW_CHUNKS):
            rows = slice(r * rc, (r + 1) * rc)
            h = pltpu.bitcast(h_ref[r * rc // 2:(r + 1) * rc // 2, :], BF16)
            p = _dot(h, wp_ref[...])
            z = _dot(h, wz_ref[...])
            sz_ref[rows, :] = _silu(z).astype(sz_ref.dtype)
            ext = jnp.concatenate([halo, p], axis=0)
            sums = _trailing_window_sums(ext, window)[MAX_WINDOW:, :]
            pooled_ref[rows, :] = (sums * (1.0 / window) - p).astype(pooled_ref.dtype)
            if r == 0:
                inv_cnt = 1.0 / jnp.minimum(pos_head, window).astype(F32)
                pooled_ref[head, :] = (sums[head, :] * inv_cnt
                                       - p[head, :]).astype(pooled_ref.dtype)
            halo = p[rc - MAX_WINDOW:, :]
        carry_ref[...] = halo

    grp = j // (group_dim // tn)
    for gi, window in enumerate(POOL_WINDOWS):
        pl.when(grp == gi)(functools.partial(body, window))


def _b_in(h, w_in, w_grp, w_out, e, seq, tm=1024, tn=512):
    m, d = 2 * h.shape[0], h.shape[1]
    nj = e // tn
    ni = m // tm
    group_dim = e // len(POOL_WINDOWS)
    wg2d = w_grp.reshape(-1, w_grp.shape[-1])
    slab_g = wg2d.shape[0] // (nj * ni)
    slab_o = w_out.shape[0] // (nj * ni)
    step = lambda j, i: (j * ni + i, 0)
    in_specs = [pl.BlockSpec((tm // 2, d), lambda j, i: (i, 0)),
                pl.BlockSpec((d, tn), lambda j, i: (0, j)),
                pl.BlockSpec((d, tn), lambda j, i: (0, nj + j)),
                pl.BlockSpec((slab_g, wg2d.shape[1]), step),
                pl.BlockSpec((slab_o, w_out.shape[1]), step)]
    out_specs = [pl.BlockSpec((tm, tn), lambda j, i: (i, j)),
                 pl.BlockSpec((tm, tn), lambda j, i: (i, j)),
                 pl.BlockSpec((slab_g, wg2d.shape[1]), step),
                 pl.BlockSpec((slab_o, w_out.shape[1]), step)]

    def outer(h_hbm, wp_hbm, wz_hbm, wgf_hbm, wof_hbm, pooled_hbm, sz_hbm, wgb_hbm, wob_hbm,
              carry_ref):
        pltpu.emit_pipeline(
            functools.partial(_b_in_step, carry_ref=carry_ref, seq=seq, group_dim=group_dim),
            grid=(nj, ni), in_specs=in_specs, out_specs=out_specs,
            dimension_semantics=("arbitrary", "arbitrary"), _explicit_indices=True,
        )(h_hbm, wp_hbm, wz_hbm, wgf_hbm, wof_hbm, pooled_hbm, sz_hbm, wgb_hbm, wob_hbm)

    in_hbm = pl.BlockSpec(memory_space=pl.ANY)
    return pl.pallas_call(
        outer,
        in_specs=[in_hbm] * 5,
        out_specs=[in_hbm] * 4,
        out_shape=[jax.ShapeDtypeStruct((m, e), BF16),
                   jax.ShapeDtypeStruct((m, e), BF16),
                   jax.ShapeDtypeStruct(wg2d.shape, BF16),
                   jax.ShapeDtypeStruct(w_out.shape, BF16)],
        scratch_shapes=[pltpu.VMEM((MAX_WINDOW, tn), F32)],
        compiler_params=pltpu.CompilerParams(vmem_limit_bytes=58 * MIB),
        name="b_in",
    )(h, w_in, w_in, wg2d, w_out)


def _b_out_kernel(pooled_ref, wg_ref, bg_ref, sc_ref, sz_ref, wo_ref, x_hbm, pg_ref,
                  o_hbm, y_buf, acc_ref, xbuf, obuf, sems, *, idx):
    def make_y(y_ref):
        mixed = _dot(pooled_ref[...], wg_ref[0]) + bg_ref[...]
        y_ref[...] = (mixed * sc_ref[...] * sz_ref[...].astype(F32)).astype(BF16)

    _out_projection_pipeline(idx, make_y, y_buf, wo_ref, x_hbm, pg_ref, o_hbm, acc_ref,
                             xbuf, obuf, sems)


def _b_out(pooled, w_grp, b_grp, scale, sz, w_out, x2d, post_gain, tm=512, tn=1024):
    m, e = pooled.shape
    d = w_out.shape[1]
    groups, gw, _ = w_grp.shape
    nn = gw // tn
    idx = _PipelineIndex(m // tm, groups * nn)
    made_rc = lambda t: idx.made(t)
    made_col = lambda t: (0, idx.made(t)[1])
    in_hbm = pl.BlockSpec(memory_space=pl.ANY)
    return pl.pallas_call(
        functools.partial(_b_out_kernel, idx=idx),
        grid=(idx.total + 1,),
        in_specs=[pl.BlockSpec((tm, gw), lambda t: (idx.made(t)[0], idx.made(t)[1] // nn)),
                  pl.BlockSpec((1, gw, tn),
                               lambda t: (idx.made(t)[1] // nn, 0, idx.made(t)[1] % nn)),
                  pl.BlockSpec((1, tn), made_col),
                  pl.BlockSpec((1, tn), made_col),
                  pl.BlockSpec((tm, tn), made_rc),
                  pl.BlockSpec((tn, d), lambda t: (idx.projected(t)[1], 0)),
                  in_hbm,
                  pl.BlockSpec((1, d), lambda t: (0, 0))],
        out_specs=in_hbm,
        out_shape=jax.ShapeDtypeStruct((m, d), F32),
        scratch_shapes=[pltpu.VMEM((2, tm, tn), BF16),
                        pltpu.VMEM((tm, d), F32),
                        pltpu.VMEM((tm, d), F32),
                        pltpu.VMEM((2, EPILOGUE_ROWS, d), F32),
                        pltpu.SemaphoreType.DMA((3, 2))],
        compiler_params=_params(("arbitrary",), 56),
        name="b_out",
    )(pooled, w_grp, b_grp.reshape(1, e), scale.reshape(1, e), sz, w_out, x2d,
      post_gain.reshape(1, d))


def kernel(x, pre_norm, post_norm, a_w_in, a_ln_g, a_ln_b, a_w_s, a_b_s, a_w_out,
           b_w_in, b_w_grp, b_b_grp, b_scale, b_w_out):
    bsz, seq, d = x.shape
    e = a_w_out.shape[1]
    x2d = x.reshape(bsz * seq, d)

    a, g, s1, s2, a_wo, b_wi = _a_in(x2d, pre_norm[0], a_w_in[0], a_w_out[0], b_w_in[0], e)
    x2d, h = _a_out(a, g, s1, s2, a_ln_g[0], a_ln_b[0], a_w_s[0], a_b_s[0],
                    a_wo, x2d, post_norm[0], pre_norm[1])

    pooled, sz, b_wg, b_wo = _b_in(h, b_wi, b_w_grp[0], b_w_out[0], e, seq)
    x2d = _b_out(pooled, b_wg.reshape(b_w_grp[0].shape), b_b_grp[0], b_scale[0], sz,
                 b_wo, x2d, post_norm[1])
    return x2d.reshape(bsz, seq, d)
```
